```python
import jax, jax.numpy as jnp
from jax import lax
import numpy as np

D_MODEL = 1024
BATCH = 8
SEQ = 2048
DEPTH = 2

CHUNK = 64
HEAD_DIM = 128
MIX_WIDTH = D_MODEL
RET_WIDTH = MIX_WIDTH // 2
SB_WIDTH = MIX_WIDTH - RET_WIDTH
N_RET_HEADS = RET_WIDTH // HEAD_DIM
N_SB_HEADS = SB_WIDTH // HEAD_DIM
D_FF = ((8 * D_MODEL // 3 + 255) // 256) * 256
SB_BLOCK = 128
ROPE_BASE = 10000.0
EPS = 1e-6
IN_WIDTHS = (RET_WIDTH, RET_WIDTH, RET_WIDTH, RET_WIDTH, SB_WIDTH, SB_WIDTH, SB_WIDTH)
IN_WIDTH = sum(IN_WIDTHS)
IN_SPLITS = tuple(int(c) for c in np.cumsum(IN_WIDTHS)[:-1])

kernel_name = "hymba_retention_stickbreaking_trunk"


def rms_norm(x, g):
    xf = x.astype(jnp.float32)
    y = xf * lax.rsqrt(jnp.mean(xf * xf, axis=-1, keepdims=True) + EPS)
    return (y * g.astype(jnp.float32)).astype(x.dtype)


def to_heads(t, n_heads):
    b, s, _ = t.shape
    return t.reshape(b, s, n_heads, HEAD_DIM).transpose(0, 2, 1, 3)


def from_heads(t):
    b, h, s, d = t.shape
    return t.transpose(0, 2, 1, 3).reshape(b, s, h * d)


def head_group_norm(x, g):
    h, d = x.shape[1], x.shape[3]
    xf = x.astype(jnp.float32)
    mu = jnp.mean(xf, axis=-1, keepdims=True)
    var = jnp.mean(jnp.square(xf - mu), axis=-1, keepdims=True)
    y = (xf - mu) * lax.rsqrt(var + EPS) * g.astype(jnp.float32).reshape(h, 1, d)
    return y.astype(x.dtype)


def head_rms_norm(x, g):
    h, d = x.shape[1], x.shape[3]
    xf = x.astype(jnp.float32)
    y = xf * lax.rsqrt(jnp.mean(xf * xf, axis=-1, keepdims=True) + EPS)
    return (y * g.astype(jnp.float32).reshape(h, 1, d)).astype(x.dtype)


def apply_rotary(x):
    s, d = x.shape[2], x.shape[3]
    inv_freq = 1.0 / (ROPE_BASE ** (jnp.arange(0, d, 2, dtype=jnp.float32) / d))
    ang = jnp.arange(s, dtype=jnp.float32)[:, None] * inv_freq[None, :]
    cos = jnp.cos(ang).astype(x.dtype)
    sin = jnp.sin(ang).astype(x.dtype)
    x1, x2 = x[..., : d // 2], x[..., d // 2:]
    return jnp.concatenate([x1 * cos - x2 * sin, x1 * sin + x2 * cos], axis=-1)


def chunk_retention(q, k, v):
    b, h, s, d = q.shape
    c = CHUNK
    n = s // c
    dt = q.dtype
    log_g = jnp.log1p(-jnp.exp2(-5.0 - jnp.arange(h, dtype=jnp.float32)))
    i = jnp.arange(c, dtype=jnp.float32)
    intra_decay = jnp.exp(log_g[:, None, None] * jnp.abs(i[:, None] - i[None, :])).astype(dt)
    q_decay = jnp.exp(log_g[:, None] * (i + 1.0)).astype(dt)[..., None]
    k_decay = jnp.exp(log_g[:, None] * (c - 1.0 - i)).astype(dt)[..., None]
    chunk_decay = jnp.exp(log_g * c).astype(dt)[None, :, None, None]
    k = k * (d ** -0.5)
    qc = q.reshape(b, h, n, c, d)
    kc = k.reshape(b, h, n, c, d)
    vc = v.reshape(b, h, n, c, v.shape[-1])
    scores = jnp.einsum('bhnid,bhnjd->bhnij', qc, kc) * intra_decay[:, None]
    intra = jnp.einsum('bhnij,bhnje->bhnie', scores, vc)

    def step(state, inp):
        q_n, k_n, v_n = inp
        cross = jnp.einsum('bhid,bhde->bhie', q_n * q_decay, state)
        state = state * chunk_decay + jnp.einsum('bhjd,bhje->bhde', k_n * k_decay, v_n)
        return state, cross

    init = jnp.zeros((b, h, d, v.shape[-1]), dtype=intra.dtype)
    xs = (jnp.moveaxis(qc, 2, 0), jnp.moveaxis(kc, 2, 0), jnp.moveaxis(vc, 2, 0))
    _, cross = lax.scan(step, init, xs)
    out = intra + jnp.moveaxis(cross, 0, 2)
    return out.reshape(b, h, s, v.shape[-1])


def stick_breaking(q, k, v):
    s, d = q.shape[2], q.shape[3]
    scale = d ** -0.5
    outs = []
    for start in range(0, s, SB_BLOCK):
        end = start + SB_BLOCK
        qb = q[:, :, start:end]
        kb = k[:, :, :end]
        vb = v[:, :, :end]
        z = jnp.einsum('bhtd,bhsd->bhts', qb, kb).astype(jnp.float32) * scale
        t_pos = jnp.arange(start, end)[:, None]
        s_pos = jnp.arange(end)[None, :]
        valid = s_pos < t_pos
        log_keep = jnp.where(valid, jax.nn.log_sigmoid(-z), 0.0)
        later = lax.cumsum(log_keep, axis=3, reverse=True) - log_keep
        w = jnp.where(valid, jnp.exp(jax.nn.log_sigmoid(z) + later), 0.0)
        outs.append(jnp.einsum('bhts,bhse->bhte', w.astype(v.dtype), vb))
    return jnp.concatenate(outs, axis=2)


def setup_inputs(seed: int = 0) -> dict:
    key = jax.random.key(seed)
    ks = jax.random.split(key, 12)
    f32 = jnp.float32

    def gain(k, shape):
        return (1.0 + 0.02 * jax.random.normal(k, shape, f32)).astype(f32)

    return {
        "x": jax.random.normal(ks[0], (BATCH, SEQ, D_MODEL), f32),
        "norm1_g": gain(ks[1], (DEPTH, D_MODEL)),
        "w_in": jax.random.normal(ks[2], (DEPTH, D_MODEL, IN_WIDTH), f32) * D_MODEL ** -0.5,
        "ret_norm_g": gain(ks[3], (DEPTH, RET_WIDTH)),
        "sb_norm_g": gain(ks[4], (DEPTH, SB_WIDTH)),
        "w_out": jax.random.normal(ks[5], (DEPTH, MIX_WIDTH, D_MODEL), f32) * MIX_WIDTH ** -0.5,
        "norm2_g": gain(ks[6], (DEPTH, D_MODEL)),
        "w_gate": jax.random.normal(ks[7], (DEPTH, D_MODEL, D_FF), f32) * D_MODEL ** -0.5,
        "w_up": jax.random.normal(ks[8], (DEPTH, D_MODEL, D_FF), f32) * D_MODEL ** -0.5,
        "w_down": jax.random.normal(ks[9], (DEPTH, D_FF, D_MODEL), f32) * D_FF ** -0.5,
        "final_g": gain(ks[10], (D_MODEL,)),
    }


def reference(x, norm1_g, w_in, ret_norm_g, sb_norm_g, w_out, norm2_g, w_gate, w_up, w_down, final_g):
    for l in range(DEPTH):
        h = rms_norm(x, norm1_g[l])
        proj = h @ w_in[l]
        rq, rk, rv, rg, sq, sk, sv = jnp.split(proj, IN_SPLITS, axis=-1)
        ret = chunk_retention(apply_rotary(to_heads(rq, N_RET_HEADS)),
                              apply_rotary(to_heads(rk, N_RET_HEADS)),
                              to_heads(rv, N_RET_HEADS))
        ret = from_heads(head_group_norm(ret, ret_norm_g[l])) * jax.nn.silu(rg)
        sb = stick_breaking(to_heads(sq, N_SB_HEADS), to_heads(sk, N_SB_HEADS), to_heads(sv, N_SB_HEADS))
        sb = from_heads(head_rms_norm(sb, sb_norm_g[l]))
        x = x + jnp.concatenate([ret, sb], axis=-1) @ w_out[l]
        h = rms_norm(x, norm2_g[l])
        x = x + (jax.nn.silu(h @ w_gate[l]) * (h @ w_up[l])) @ w_down[l]
    return rms_norm(x, final_g)
```

```python
import functools

import jax
import jax.numpy as jnp
import numpy as np
from jax import lax
from jax.experimental import pallas as pl
from jax.experimental.pallas import tpu as pltpu

HEAD_DIM = 128
ROPE_BASE = 10000.0
EPS = 1e-6
RET_BLOCK = 128
RET_CHUNK = 64
SB_TILE = 128
VMEM_LIMIT = 48 * 1024 * 1024

F32 = jnp.float32
BF16 = jnp.bfloat16


def _sigmoid(x):
    return 1.0 / (1.0 + jnp.exp(-x))


def _inproj_kernel(x_ref, g_ref, w_ref, cos_ref, sin_ref, o_ref, h_ref, *, n_rot_tiles):
    j = pl.program_id(1)

    @pl.when(j == 0)
    def _():
        xf = x_ref[...]
        y = xf * lax.rsqrt(jnp.mean(xf * xf, axis=-1, keepdims=True) + EPS)
        h_ref[...] = (y * g_ref[...]).astype(BF16)

    acc = jnp.dot(h_ref[...], w_ref[...], preferred_element_type=F32)

    @pl.when(j < n_rot_tiles)
    def _():
        cos = cos_ref[...]
        sin = sin_ref[...]
        for hh in range(acc.shape[1] // HEAD_DIM):
            a = acc[:, hh * HEAD_DIM:(hh + 1) * HEAD_DIM]
            rot = pltpu.roll(a, HEAD_DIM // 2, 1)
            o_ref[:, hh * HEAD_DIM:(hh + 1) * HEAD_DIM] = (a * cos + rot * sin).astype(BF16)

    @pl.when(j >= n_rot_tiles)
    def _():
        o_ref[...] = acc.astype(BF16)


def _inproj(x2d, g, w, cos_t, sin_t, *, seq, n_rot_cols, tm=1024, tn=512):
    t, d = x2d.shape
    n = w.shape[1]
    assert t % tm == 0 and n % tn == 0 and seq % tm == 0 and n_rot_cols % tn == 0
    pos_blocks = seq // tm
    return pl.pallas_call(
        functools.partial(_inproj_kernel, n_rot_tiles=n_rot_cols // tn),
        grid=(t // tm, n // tn),
        in_specs=[
            pl.BlockSpec((tm, d), lambda i, j: (i, 0)),
            pl.BlockSpec((1, d), lambda i, j: (0, 0)),
            pl.BlockSpec((d, tn), lambda i, j: (0, j)),
            pl.BlockSpec((tm, HEAD_DIM), lambda i, j: (i % pos_blocks, 0)),
            pl.BlockSpec((tm, HEAD_DIM), lambda i, j: (i % pos_blocks, 0)),
        ],
        out_specs=pl.BlockSpec((tm, tn), lambda i, j: (i, j)),
        out_shape=jax.ShapeDtypeStruct((t, n), BF16),
        scratch_shapes=[pltpu.VMEM((tm, d), BF16)],
        compiler_params=pltpu.CompilerParams(
            dimension_semantics=("arbitrary", "arbitrary"), vmem_limit_bytes=VMEM_LIMIT),
        name="inproj",
    )(x2d, g, w, cos_t, sin_t)


def _ret_kernel(q_ref, k_ref, v_ref, gate_ref, dmat_ref, qdec_ref, kdec_ref, sdec_ref, gn_ref,
                o_ref, *, n_blocks):
    dmat = dmat_ref[...]
    qdec = qdec_ref[...]
    kdec = kdec_ref[...]
    sdec = sdec_ref[...]
    gn = gn_ref[...]

    def body(n, state):
        r = pl.multiple_of(n * RET_BLOCK, RET_BLOCK)
        q = q_ref[pl.ds(r, RET_BLOCK), :]
        k = k_ref[pl.ds(r, RET_BLOCK), :]
        v = v_ref[pl.ds(r, RET_BLOCK), :]
        s = lax.dot_general(q, k, (((1,), (1,)), ((), ())), preferred_element_type=F32) * dmat
        intra = jnp.dot(s.astype(BF16), v, preferred_element_type=F32)
        qd = (q.astype(F32) * qdec).astype(BF16)
        cross = jnp.dot(qd, state.astype(BF16), preferred_element_type=F32)
        kd = (k.astype(F32) * kdec).astype(BF16)
        kv = lax.dot_general(kd, v, (((0,), (0,)), ((), ())), preferred_element_type=F32)
        o = intra + cross
        mu = jnp.mean(o, axis=-1, keepdims=True)
        oc = o - mu
        var = jnp.mean(oc * oc, axis=-1, keepdims=True)
        y = oc * lax.rsqrt(var + EPS) * gn
        gate = gate_ref[pl.ds(r, RET_BLOCK), :].astype(F32)
        o_ref[pl.ds(r, RET_BLOCK), :] = (y * (gate * _sigmoid(gate))).astype(BF16)
        return state * sdec + kv

    lax.fori_loop(0, n_blocks, body, jnp.zeros((HEAD_DIM, HEAD_DIM), F32))


def _retention(proj, dmat, qdec, kdec, sdec, gn, *, n_heads, col_q, col_k, col_v, col_g):
    b, s, _ = proj.shape
    hd = HEAD_DIM

    def col(c0):
        return pl.BlockSpec((None, s, hd), lambda bi, hi: (bi, 0, c0 // hd + hi))

    def per_head(rows):
        return pl.BlockSpec((None, rows, hd), lambda bi, hi: (hi, 0, 0))

    return pl.pallas_call(
        functools.partial(_ret_kernel, n_blocks=s // RET_BLOCK),
        grid=(b, n_heads),
        in_specs=[col(col_q), col(col_k), col(col_v), col(col_g),
                  per_head(RET_BLOCK), per_head(RET_BLOCK), per_head(RET_BLOCK),
                  per_head(1), per_head(1)],
        out_specs=pl.BlockSpec((None, s, hd), lambda bi, hi: (bi, 0, hi)),
        out_shape=jax.ShapeDtypeStruct((b, s, n_heads * hd), BF16),
        compiler_params=pltpu.CompilerParams(
            dimension_semantics=("arbitrary", "arbitrary"), vmem_limit_bytes=VMEM_LIMIT),
        name="retention",
    )(proj, proj, proj, proj, dmat, qdec, kdec, sdec, gn)


def _sb_kernel(q_ref, k_ref, v_ref, gn_ref, o_ref, *, scale):
    qi = pl.program_id(2)
    tq = SB_TILE
    q = q_ref[...]
    row = lax.broadcasted_iota(jnp.int32, (tq, tq), 0)
    colm = lax.broadcasted_iota(jnp.int32, (tq, tq), 1)
    tri = jnp.where(row > colm, 1.0, 0.0).astype(BF16)
    valid = colm < row

    def tile(j, carry, acc, masked):
        r = pl.multiple_of(j * tq, tq)
        kj = k_ref[pl.ds(r, tq), :]
        vj = v_ref[pl.ds(r, tq), :]
        z = lax.dot_general(q, kj, (((1,), (1,)), ((), ())), preferred_element_type=F32) * scale
        sp = jnp.maximum(z, 0.0) + jnp.log1p(jnp.exp(-jnp.abs(z)))
        lk = -sp
        if masked:
            lk = jnp.where(valid, lk, 0.0)
        hi = lk.astype(BF16)
        lo = (lk - hi.astype(F32)).astype(BF16)
        later = (jnp.dot(hi, tri, preferred_element_type=F32)
                 + jnp.dot(lo, tri, preferred_element_type=F32)) + carry
        w = jnp.exp((z - sp) + later)
        if masked:
            w = jnp.where(valid, w, 0.0)
        acc = acc + jnp.dot(w.astype(BF16), vj, preferred_element_type=F32)
        carry = carry + jnp.sum(lk, axis=-1, keepdims=True)
        return carry, acc

    carry, acc = tile(qi, jnp.zeros((tq, 1), F32), jnp.zeros((tq, HEAD_DIM), F32), True)

    def body(it, ca):
        return tile(qi - 1 - it, ca[0], ca[1], False)

    carry, acc = lax.fori_loop(0, qi, body, (carry, acc))
    y = acc * lax.rsqrt(jnp.mean(acc * acc, axis=-1, keepdims=True) + EPS)
    o_ref[...] = (y * gn_ref[...]).astype(BF16)


def _stick_breaking(proj, gn, *, n_heads, col_q, col_k, col_v):
    b, s, _ = proj.shape
    hd = HEAD_DIM
    tq = SB_TILE
    return pl.pallas_call(
        functools.partial(_sb_kernel, scale=hd ** -0.5),
        grid=(b, n_heads, s // tq),
        in_specs=[
            pl.BlockSpec((None, tq, hd), lambda bi, hi, qi: (bi, qi, col_q // hd + hi)),
            pl.BlockSpec((None, s, hd), lambda bi, hi, qi: (bi, 0, col_k // hd + hi)),
            pl.BlockSpec((None, s, hd), lambda bi, hi, qi: (bi, 0, col_v // hd + hi)),
            pl.BlockSpec((None, 1, hd), lambda bi, hi, qi: (hi, 0, 0)),
        ],
        out_specs=pl.BlockSpec((None, tq, hd), lambda bi, hi, qi: (bi, qi, hi)),
        out_shape=jax.ShapeDtypeStruct((b, s, n_heads * hd), BF16),
        compiler_params=pltpu.CompilerParams(
            dimension_semantics=("arbitrary", "arbitrary", "arbitrary"),
            vmem_limit_bytes=VMEM_LIMIT),
        name="stick_breaking",
    )(proj, proj, proj, gn)


def _outproj_kernel(x_ref, ret_ref, sb_ref, wa_ref, wb_ref, g_ref, x1_ref, h_ref):
    x1 = (x_ref[...]
          + jnp.dot(ret_ref[...], wa_ref[...], preferred_element_type=F32)
          + jnp.dot(sb_ref[...], wb_ref[...], preferred_element_type=F32))
    x1_ref[...] = x1
    y = x1 * lax.rsqrt(jnp.mean(x1 * x1, axis=-1, keepdims=True) + EPS)
    h_ref[...] = (y * g_ref[...]).astype(BF16)


def _outproj(x2d, ret2d, sb2d, wa, wb, g, *, tm=512):
    t, d = x2d.shape
    ka, kb = wa.shape[0], wb.shape[0]
    assert t % tm == 0
    return pl.pallas_call(
        _outproj_kernel,
        grid=(t // tm,),
        in_specs=[
            pl.BlockSpec((tm, d), lambda i: (i, 0)),
            pl.BlockSpec((tm, ka), lambda i: (i, 0)),
            pl.BlockSpec((tm, kb), lambda i: (i, 0)),
            pl.BlockSpec((ka, d), lambda i: (0, 0)),
            pl.BlockSpec((kb, d), lambda i: (0, 0)),
            pl.BlockSpec((1, d), lambda i: (0, 0)),
        ],
        out_specs=[pl.BlockSpec((tm, d), lambda i: (i, 0)),
                   pl.BlockSpec((tm, d), lambda i: (i, 0))],
        out_shape=[jax.ShapeDtypeStruct((t, d), F32), jax.ShapeDtypeStruct((t, d), BF16)],
        compiler_params=pltpu.CompilerParams(
            dimension_semantics=("arbitrary",), vmem_limit_bytes=VMEM_LIMIT),
        name="outproj",
    )(x2d, ret2d, sb2d, wa, wb, g)


def _ffn_kernel(h_ref, x1_ref, wg_ref, wu_ref, wd_ref, fg_ref, o_ref, acc_ref, *, final_norm):
    f = pl.program_id(1)

    @pl.when(f == 0)
    def _():
        acc_ref[...] = x1_ref[...]

    h = h_ref[...]
    gt = jnp.dot(h, wg_ref[...], preferred_element_type=F32)
    up = jnp.dot(h, wu_ref[...], preferred_element_type=F32)
    a = (gt * _sigmoid(gt) * up).astype(BF16)
    acc_ref[...] += jnp.dot(a, wd_ref[...], preferred_element_type=F32)

    @pl.when(f == pl.num_programs(1) - 1)
    def _():
        x2 = acc_ref[...]
        if final_norm:
            y = x2 * lax.rsqrt(jnp.mean(x2 * x2, axis=-1, keepdims=True) + EPS)
            o_ref[...] = y * fg_ref[...]
        else:
            o_ref[...] = x2


def _ffn(h2d, x1, wg, wu, wd, fg, *, final_norm, tm=1024, tf=256):
    t, d = x1.shape
    ff = wg.shape[1]
    assert t % tm == 0 and ff % tf == 0
    return pl.pallas_call(
        functools.partial(_ffn_kernel, final_norm=final_norm),
        grid=(t // tm, ff // tf),
        in_specs=[
            pl.BlockSpec((tm, d), lambda i, f: (i, 0)),
            pl.BlockSpec((tm, d), lambda i, f: (i, 0)),
            pl.BlockSpec((d, tf), lambda i, f: (0, f)),
            pl.BlockSpec((d, tf), lambda i, f: (0, f)),
            pl.BlockSpec((tf, d), lambda i, f: (f, 0)),
            pl.BlockSpec((1, d), lambda i, f: (0, 0)),
        ],
        out_specs=pl.BlockSpec((tm, d), lambda i, f: (i, 0)),
        out_shape=jax.ShapeDtypeStruct((t, d), F32),
        scratch_shapes=[pltpu.VMEM((tm, d), F32)],
        compiler_params=pltpu.CompilerParams(
            dimension_semantics=("arbitrary", "arbitrary"), vmem_limit_bytes=VMEM_LIMIT),
        name="ffn",
    )(h2d, x1, wg, wu, wd, fg)


def _rotary_tables(seq):
    d = HEAD_DIM
    inv_freq = 1.0 / (ROPE_BASE ** (jnp.arange(0, d, 2, dtype=F32) / d))
    ang = jnp.arange(seq, dtype=F32)[:, None] * inv_freq[None, :]
    cos = jnp.cos(ang)
    sin = jnp.sin(ang)
    return jnp.concatenate([cos, cos], axis=-1), jnp.concatenate([-sin, sin], axis=-1)


def _retention_tables(n_heads):
    c = RET_BLOCK
    scale = HEAD_DIM ** -0.5
    log_g = jnp.log1p(-jnp.exp2(-5.0 - jnp.arange(n_heads, dtype=F32)))
    i = jnp.arange(c, dtype=F32)
    diff = i[:, None] - i[None, :]
    ci = jnp.arange(c) // RET_CHUNK
    same = ci[:, None] == ci[None, :]
    earlier = ci[None, :] < ci[:, None]
    dist = jnp.where(same, jnp.abs(diff), diff)
    dmat = jnp.where((same | earlier)[None], jnp.exp(log_g[:, None, None] * dist[None]), 0.0) * scale
    ones = jnp.ones((1, 1, HEAD_DIM), F32)
    qdec = jnp.exp(log_g[:, None] * (i + 1.0))[..., None] * ones
    kdec = jnp.exp(log_g[:, None] * (c - 1.0 - i))[..., None] * scale * ones
    sdec = jnp.exp(log_g * c)[:, None, None] * ones
    return dmat.astype(F32), qdec, kdec, sdec


def kernel(x, norm1_g, w_in, ret_norm_g, sb_norm_g, w_out, norm2_g, w_gate, w_up, w_down, final_g):
    b, s, d = x.shape
    depth = w_in.shape[0]
    ret_w = ret_norm_g.shape[1]
    sb_w = sb_norm_g.shape[1]
    n_ret, n_sb = ret_w // HEAD_DIM, sb_w // HEAD_DIM
    c_rq, c_rk, c_rv, c_rg = 0, ret_w, 2 * ret_w, 3 * ret_w
    c_sq, c_sk, c_sv = 4 * ret_w, 4 * ret_w + sb_w, 4 * ret_w + 2 * sb_w

    cos_t, sin_t = _rotary_tables(s)
    dmat, qdec, kdec, sdec = _retention_tables(n_ret)

    xc = x.reshape(b * s, d)
    for l in range(depth):
        proj = _inproj(xc, norm1_g[l][None], w_in[l].astype(BF16), cos_t, sin_t,
                       seq=s, n_rot_cols=2 * ret_w)
        proj = proj.reshape(b, s, -1)
        ret = _retention(proj, dmat, qdec, kdec, sdec, ret_norm_g[l].reshape(n_ret, 1, HEAD_DIM),
                         n_heads=n_ret, col_q=c_rq, col_k=c_rk, col_v=c_rv, col_g=c_rg)
        sb = _stick_breaking(proj, sb_norm_g[l].reshape(n_sb, 1, HEAD_DIM),
                             n_heads=n_sb, col_q=c_sq, col_k=c_sk, col_v=c_sv)
        wo = w_out[l].astype(BF16)
        x1, h2 = _outproj(xc, ret.reshape(b * s, ret_w), sb.reshape(b * s, sb_w),
                          wo[:ret_w], wo[ret_w:], norm2_g[l][None])
        xc = _ffn(h2, x1, w_gate[l].astype(BF16), w_up[l].astype(BF16), w_down[l].astype(BF16),
                  final_g[None], final_norm=(l == depth - 1))
    return xc.reshape(b, s, d)
```

```python
import functools

import jax
import jax.numpy as jnp
from jax import lax
from jax.experimental import pallas as pl
from jax.experimental.pallas import tpu as pltpu

HEAD_DIM = 128
ROPE_BASE = 10000.0
EPS = 1e-6
LOG2_E = 1.4426950408889634
RET_BLOCK = 128
RET_CHUNK = 64
RET_UNROLL = 2
SB_TILE = 256
FFN_TILE = 256
VMEM_LIMIT = 48 * 1024 * 1024

F32 = jnp.float32
BF16 = jnp.bfloat16
_NT = (((1,), (1,)), ((), ()))
_TN = (((0,), (0,)), ((), ()))


def _sigmoid(x):
    return 1.0 / (1.0 + jnp.exp(-x))


def _inproj_kernel(x_ref, g_ref, w_ref, cos_ref, sin_ref, o_ref, h_ref, *,
                   n_rot_tiles, scaled_tile, scale):
    j = pl.program_id(1)

    @pl.when(j == 0)
    def _():
        xf = x_ref[...]
        y = xf * lax.rsqrt(jnp.mean(xf * xf, axis=-1, keepdims=True) + EPS)
        h_ref[...] = (y * g_ref[...]).astype(BF16)

    acc = jnp.dot(h_ref[...], w_ref[...], preferred_element_type=F32)

    @pl.when(j < n_rot_tiles)
    def _():
        cos = cos_ref[...]
        sin = sin_ref[...]
        for hh in range(acc.shape[1] // HEAD_DIM):
            a = acc[:, hh * HEAD_DIM:(hh + 1) * HEAD_DIM]
            rot = pltpu.roll(a, HEAD_DIM // 2, 1)
            o_ref[:, hh * HEAD_DIM:(hh + 1) * HEAD_DIM] = (a * cos + rot * sin).astype(BF16)

    @pl.when(j == scaled_tile)
    def _():
        o_ref[...] = (acc * scale).astype(BF16)

    @pl.when((j >= n_rot_tiles) & (j != scaled_tile))
    def _():
        o_ref[...] = acc.astype(BF16)


def _inproj(x2d, g, w, cos_t, sin_t, *, seq, n_rot_cols, scaled_col, tm=1024, tn=512):
    t, d = x2d.shape
    n = w.shape[1]
    assert t % tm == 0 and n % tn == 0 and seq % tm == 0
    assert n_rot_cols % tn == 0 and scaled_col % tn == 0
    pos_blocks = seq // tm
    return pl.pallas_call(
        functools.partial(_inproj_kernel, n_rot_tiles=n_rot_cols // tn,
                          scaled_tile=scaled_col // tn, scale=HEAD_DIM ** -0.5 * LOG2_E),
        grid=(t // tm, n // tn),
        in_specs=[
            pl.BlockSpec((tm, d), lambda i, j: (i, 0)),
            pl.BlockSpec((1, d), lambda i, j: (0, 0)),
            pl.BlockSpec((d, tn), lambda i, j: (0, j)),
            pl.BlockSpec((tm, HEAD_DIM), lambda i, j: (i % pos_blocks, 0)),
            pl.BlockSpec((tm, HEAD_DIM), lambda i, j: (i % pos_blocks, 0)),
        ],
        out_specs=pl.BlockSpec((tm, tn), lambda i, j: (i, j)),
        out_shape=jax.ShapeDtypeStruct((t, n), BF16),
        scratch_shapes=[pltpu.VMEM((tm, d), BF16)],
        compiler_params=pltpu.CompilerParams(
            dimension_semantics=("arbitrary", "arbitrary"), vmem_limit_bytes=VMEM_LIMIT),
        name="inproj",
    )(x2d, g, w, cos_t, sin_t)


def _ret_kernel(q_ref, k_ref, v_ref, gate_ref, dmat_ref, qdec_ref, kdec_ref, sdec_ref, gn_ref,
                o_ref, *, n_heads, n_trips):
    hd = HEAD_DIM

    def step(r, hh, state):
        c = slice(hh * hd, (hh + 1) * hd)
        q = q_ref[pl.ds(r, RET_BLOCK), c]
        k = k_ref[pl.ds(r, RET_BLOCK), c]
        v = v_ref[pl.ds(r, RET_BLOCK), c]
        s = lax.dot_general(q, k, _NT, preferred_element_type=F32) * dmat_ref[hh]
        intra = jnp.dot(s.astype(BF16), v, preferred_element_type=F32)
        qd = (q.astype(F32) * qdec_ref[hh]).astype(BF16)
        cross = jnp.dot(qd, state.astype(BF16), preferred_element_type=F32)
        kd = (k.astype(F32) * kdec_ref[hh]).astype(BF16)
        kv = lax.dot_general(kd, v, _TN, preferred_element_type=F32)
        o = intra + cross
        mu = jnp.mean(o, axis=-1, keepdims=True)
        oc = o - mu
        var = jnp.mean(oc * oc, axis=-1, keepdims=True)
        y = oc * lax.rsqrt(var + EPS) * gn_ref[hh]
        gate = gate_ref[pl.ds(r, RET_BLOCK), c].astype(F32)
        o_ref[pl.ds(r, RET_BLOCK), c] = (y * (gate * _sigmoid(gate))).astype(BF16)
        return state * sdec_ref[hh] + kv

    def body(n, states):
        states = list(states)
        for u in range(RET_UNROLL):
            r = pl.multiple_of((n * RET_UNROLL + u) * RET_BLOCK, RET_BLOCK)
            for hh in range(n_heads):
                states[hh] = step(r, hh, states[hh])
        return tuple(states)

    lax.fori_loop(0, n_trips, body,
                  tuple(jnp.zeros((hd, hd), F32) for _ in range(n_heads)))


def _retention(proj, dmat, qdec, kdec, sdec, gn, *, n_heads, col_q, col_k, col_v, col_g):
    b, s, _ = proj.shape
    w = n_heads * HEAD_DIM
    assert s % (RET_BLOCK * RET_UNROLL) == 0
    assert col_q % w == 0 and col_k % w == 0 and col_v % w == 0 and col_g % w == 0

    def col(c0):
        return pl.BlockSpec((None, s, w), lambda bi: (bi, 0, c0 // w))

    def table(a):
        return pl.BlockSpec(a.shape, lambda bi: (0, 0, 0))

    return pl.pallas_call(
        functools.partial(_ret_kernel, n_heads=n_heads, n_trips=s // (RET_BLOCK * RET_UNROLL)),
        grid=(b,),
        in_specs=[col(col_q), col(col_k), col(col_v), col(col_g),
                  table(dmat), table(qdec), table(kdec), table(sdec), table(gn)],
        out_specs=pl.BlockSpec((None, s, w), lambda bi: (bi, 0, 0)),
        out_shape=jax.ShapeDtypeStruct((b, s, w), BF16),
        compiler_params=pltpu.CompilerParams(
            dimension_semantics=("arbitrary",), vmem_limit_bytes=VMEM_LIMIT),
        name="retention",
    )(proj, proj, proj, proj, dmat, qdec, kdec, sdec, gn)


def _sb_kernel(q_ref, k_ref, v_ref, tri_ref, gn_ref, o_ref, acc_ref, carry_ref, *, n_heads):
    qi = pl.program_id(1)
    t = SB_TILE
    hd = HEAD_DIM
    row = lax.broadcasted_iota(jnp.int32, (t, t), 0)
    colm = lax.broadcasted_iota(jnp.int32, (t, t), 1)
    valid = colm < row

    def tile(j, masked):
        r = pl.multiple_of(j * t, t)
        heads = [slice(hh * hd, (hh + 1) * hd) for hh in range(n_heads)]
        hilo, logb, rows = [], [], []
        for c in heads:
            z = lax.dot_general(q_ref[:, c], k_ref[pl.ds(r, t), c], _NT,
                                preferred_element_type=F32)
            sp = jnp.maximum(z, 0.0) + jnp.log2(1.0 + jnp.exp2(-jnp.abs(z)))
            spm = jnp.where(valid, sp, 0.0) if masked else sp
            hi = spm.astype(BF16)
            lo = (spm - hi.astype(F32)).astype(BF16)
            hilo.append(jnp.concatenate([hi, lo], axis=1))
            logb.append(z - sp)
            rows.append(jnp.sum(spm, axis=-1, keepdims=True))
        ws = []
        for hh in range(n_heads):
            after = jnp.dot(hilo[hh], tri_ref[...], preferred_element_type=F32)
            w = jnp.exp2(logb[hh] - after)
            if masked:
                w = jnp.where(valid, w, 0.0)
            ws.append(w.astype(BF16))
        for hh, c in enumerate(heads):
            pv = jnp.dot(ws[hh], v_ref[pl.ds(r, t), c], preferred_element_type=F32)
            if masked:
                acc_ref[:, c] = pv
                carry_ref[:, c] = jnp.broadcast_to(rows[hh], (t, hd))
            else:
                carry = carry_ref[:, c]
                acc_ref[:, c] += jnp.exp2(-carry) * pv
                carry_ref[:, c] = carry + rows[hh]

    tile(qi, True)

    def body(it, _):
        tile(qi - 1 - it, False)
        return 0

    lax.fori_loop(0, qi, body, 0)

    for hh in range(n_heads):
        c = slice(hh * hd, (hh + 1) * hd)
        acc = acc_ref[:, c]
        y = acc * lax.rsqrt(jnp.mean(acc * acc, axis=-1, keepdims=True) + EPS)
        o_ref[:, c] = (y * gn_ref[:, c]).astype(BF16)


def _stick_breaking(proj, tri, gn, *, n_heads, col_q, col_k, col_v):
    b, s, _ = proj.shape
    w = n_heads * HEAD_DIM
    t = SB_TILE
    assert s % t == 0 and col_q % w == 0 and col_k % w == 0 and col_v % w == 0
    return pl.pallas_call(
        functools.partial(_sb_kernel, n_heads=n_heads),
        grid=(b, s // t),
        in_specs=[
            pl.BlockSpec((None, t, w), lambda bi, qi: (bi, qi, col_q // w)),
            pl.BlockSpec((None, s, w), lambda bi, qi: (bi, 0, col_k // w)),
            pl.BlockSpec((None, s, w), lambda bi, qi: (bi, 0, col_v // w)),
            pl.BlockSpec(tri.shape, lambda bi, qi: (0, 0)),
            pl.BlockSpec((1, w), lambda bi, qi: (0, 0)),
        ],
        out_specs=pl.BlockSpec((None, t, w), lambda bi, qi: (bi, qi, 0)),
        out_shape=jax.ShapeDtypeStruct((b, s, w), BF16),
        scratch_shapes=[pltpu.VMEM((t, w), F32), pltpu.VMEM((t, w), F32)],
        compiler_params=pltpu.CompilerParams(
            dimension_semantics=("arbitrary", "arbitrary"), vmem_limit_bytes=VMEM_LIMIT),
        name="stick_breaking",
    )(proj, proj, proj, tri, gn)


def _mix_ffn_kernel(x_ref, ret_ref, sb_ref, wo_ref, g2_ref, wgu_ref, wd_ref, fg_ref,
                    o_ref, acc_ref, h_ref, *, final_norm):
    f = pl.program_id(1)
    ka = ret_ref.shape[1]
    tf = wd_ref.shape[0]

    @pl.when(f == 0)
    def _():
        x1 = (x_ref[...]
              + jnp.dot(ret_ref[...], wo_ref[:ka, :], preferred_element_type=F32)
              + jnp.dot(sb_ref[...], wo_ref[ka:, :], preferred_element_type=F32))
        acc_ref[...] = x1
        y = x1 * lax.rsqrt(jnp.mean(x1 * x1, axis=-1, keepdims=True) + EPS)
        h_ref[...] = (y * g2_ref[...]).astype(BF16)

    gu = jnp.dot(h_ref[...], wgu_ref[...], preferred_element_type=F32)
    gt = gu[:, :tf]
    a = (gt * _sigmoid(gt) * gu[:, tf:]).astype(BF16)
    acc_ref[...] += jnp.dot(a, wd_ref[...], preferred_element_type=F32)

    @pl.when(f == pl.num_programs(1) - 1)
    def _():
        x2 = acc_ref[...]
        if final_norm:
            y = x2 * lax.rsqrt(jnp.mean(x2 * x2, axis=-1, keepdims=True) + EPS)
            o_ref[...] = y * fg_ref[...]
        else:
            o_ref[...] = x2


def _mix_ffn(x2d, ret2d, sb2d, wo, g2, wgu, wd, fg, *, final_norm, tm=1024):
    t, d = x2d.shape
    ka, kb = ret2d.shape[1], sb2d.shape[1]
    n_f, _, tf2 = wgu.shape
    tf = tf2 // 2
    assert t % tm == 0 and wd.shape[0] == n_f * tf and wo.shape[0] == ka + kb
    return pl.pallas_call(
        functools.partial(_mix_ffn_kernel, final_norm=final_norm),
        grid=(t // tm, n_f),
        in_specs=[
            pl.BlockSpec((tm, d), lambda i, f: (i, 0)),
            pl.BlockSpec((tm, ka), lambda i, f: (i, 0)),
            pl.BlockSpec((tm, kb), lambda i, f: (i, 0)),
            pl.BlockSpec((ka + kb, d), lambda i, f: (0, 0)),
            pl.BlockSpec((1, d), lambda i, f: (0, 0)),
            pl.BlockSpec((None, d, tf2), lambda i, f: (f, 0, 0)),
            pl.BlockSpec((tf, d), lambda i, f: (f, 0)),
            pl.BlockSpec((1, d), lambda i, f: (0, 0)),
        ],
        out_specs=pl.BlockSpec((tm, d), lambda i, f: (i, 0)),
        out_shape=jax.ShapeDtypeStruct((t, d), F32),
        scratch_shapes=[pltpu.VMEM((tm, d), F32), pltpu.VMEM((tm, d), BF16)],
        compiler_params=pltpu.CompilerParams(
            dimension_semantics=("arbitrary", "arbitrary"), vmem_limit_bytes=VMEM_LIMIT),
        name="mix_ffn",
    )(x2d, ret2d, sb2d, wo, g2, wgu, wd, fg)


def _gate_up_tiles(wg, wu, tf):
    d, ff = wg.shape
    assert ff % tf == 0
    both = jnp.concatenate([wg.reshape(d, ff // tf, tf), wu.reshape(d, ff // tf, tf)], axis=2)
    return both.transpose(1, 0, 2).astype(BF16)


def _rotary_tables(seq):
    d = HEAD_DIM
    inv_freq = 1.0 / (ROPE_BASE ** (jnp.arange(0, d, 2, dtype=F32) / d))
    ang = jnp.arange(seq, dtype=F32)[:, None] * inv_freq[None, :]
    cos = jnp.cos(ang)
    sin = jnp.sin(ang)
    return jnp.concatenate([cos, cos], axis=-1), jnp.concatenate([-sin, sin], axis=-1)


def _retention_tables(n_heads):
    c = RET_BLOCK
    scale = HEAD_DIM ** -0.5
    log_g = jnp.log1p(-jnp.exp2(-5.0 - jnp.arange(n_heads, dtype=F32)))
    i = jnp.arange(c, dtype=F32)
    diff = i[:, None] - i[None, :]
    ci = jnp.arange(c) // RET_CHUNK
    same = ci[:, None] == ci[None, :]
    earlier = ci[None, :] < ci[:, None]
    dist = jnp.where(same, jnp.abs(diff), diff)
    dmat = jnp.where((same | earlier)[None], jnp.exp(log_g[:, None, None] * dist[None]), 0.0) * scale
    ones = jnp.ones((1, 1, HEAD_DIM), F32)
    qdec = jnp.exp(log_g[:, None] * (i + 1.0))[..., None] * ones
    kdec = jnp.exp(log_g[:, None] * (c - 1.0 - i))[..., None] * scale * ones
    sdec = jnp.exp(log_g * c)[:, None, None] * ones
    return dmat.astype(F32), qdec, kdec, sdec


def _after_matrix():
    j = jnp.arange(SB_TILE)
    tri = (j[:, None] > j[None, :]).astype(BF16)
    return jnp.concatenate([tri, tri], axis=0)


def kernel(x, norm1_g, w_in, ret_norm_g, sb_norm_g, w_out, norm2_g, w_gate, w_up, w_down, final_g):
    b, s, d = x.shape
    depth = w_in.shape[0]
    ret_w = ret_norm_g.shape[1]
    sb_w = sb_norm_g.shape[1]
    n_ret, n_sb = ret_w // HEAD_DIM, sb_w // HEAD_DIM
    c_rq, c_rk, c_rv, c_rg = 0, ret_w, 2 * ret_w, 3 * ret_w
    c_sq, c_sk, c_sv = 4 * ret_w, 4 * ret_w + sb_w, 4 * ret_w + 2 * sb_w

    cos_t, sin_t = _rotary_tables(s)
    dmat, qdec, kdec, sdec = _retention_tables(n_ret)
    tri = _after_matrix()

    xc = x.reshape(b * s, d)
    for l in range(depth):
        proj = _inproj(xc, norm1_g[l][None], w_in[l].astype(BF16), cos_t, sin_t,
                       seq=s, n_rot_cols=2 * ret_w, scaled_col=c_sq)
        proj = proj.reshape(b, s, -1)
        ret = _retention(proj, dmat, qdec, kdec, sdec, ret_norm_g[l].reshape(n_ret, 1, HEAD_DIM),
                         n_heads=n_ret, col_q=c_rq, col_k=c_rk, col_v=c_rv, col_g=c_rg)
        sb = _stick_breaking(proj, tri, sb_norm_g[l][None],
                             n_heads=n_sb, col_q=c_sq, col_k=c_sk, col_v=c_sv)
        xc = _mix_ffn(xc, ret.reshape(b * s, ret_w), sb.reshape(b * s, sb_w),
                      w_out[l].astype(BF16), norm2_g[l][None],
                      _gate_up_tiles(w_gate[l], w_up[l], FFN_TILE), w_down[l].astype(BF16),
                      final_g[None], final_norm=(l == depth - 1))
    return xc.reshape(b, s, d)
```

```python
import functools

import jax
import jax.numpy as jnp
from jax import lax
from jax.experimental import pallas as pl
from jax.experimental.pallas import tpu as pltpu

HEAD_DIM = 128
ROPE_BASE = 10000.0
EPS = 1e-6
LOG2_E = 1.4426950408889634
RET_BLOCK = 128
RET_CHUNK = 64
RET_UNROLL = 2
SB_TILE = 256
FFN_TILE = 256
SB_DEAD_LOG2 = 151.0
VMEM_LIMIT = 48 * 1024 * 1024

F32 = jnp.float32
BF16 = jnp.bfloat16
_NT = (((1,), (1,)), ((), ()))
_TN = (((0,), (0,)), ((), ()))


def _sigmoid(x):
    return 1.0 / (1.0 + jnp.exp(-x))


def _inproj_kernel(x_ref, g_ref, w_ref, cos_ref, sin_ref, o_ref, h_ref, *,
                   n_rot_tiles, scaled_tile, scale):
    j = pl.program_id(1)

    @pl.when(j == 0)
    def _():
        xf = x_ref[...]
        y = xf * lax.rsqrt(jnp.mean(xf * xf, axis=-1, keepdims=True) + EPS)
        h_ref[...] = (y * g_ref[...]).astype(BF16)

    acc = jnp.dot(h_ref[...], w_ref[...], preferred_element_type=F32)

    @pl.when(j < n_rot_tiles)
    def _():
        cos = cos_ref[...]
        sin = sin_ref[...]
        for hh in range(acc.shape[1] // HEAD_DIM):
            a = acc[:, hh * HEAD_DIM:(hh + 1) * HEAD_DIM]
            rot = pltpu.roll(a, HEAD_DIM // 2, 1)
            o_ref[:, hh * HEAD_DIM:(hh + 1) * HEAD_DIM] = (a * cos + rot * sin).astype(BF16)

    @pl.when(j == scaled_tile)
    def _():
        o_ref[...] = (acc * scale).astype(BF16)

    @pl.when((j >= n_rot_tiles) & (j != scaled_tile))
    def _():
        o_ref[...] = acc.astype(BF16)


def _inproj(x2d, g, w, cos_t, sin_t, *, seq, n_rot_cols, scaled_col, tm=1024, tn=512):
    t, d = x2d.shape
    n = w.shape[1]
    assert t % tm == 0 and n % tn == 0 and seq % tm == 0
    assert n_rot_cols % tn == 0 and scaled_col % tn == 0
    pos_blocks = seq // tm
    return pl.pallas_call(
        functools.partial(_inproj_kernel, n_rot_tiles=n_rot_cols // tn,
                          scaled_tile=scaled_col // tn, scale=HEAD_DIM ** -0.5 * LOG2_E),
        grid=(t // tm, n // tn),
        in_specs=[
            pl.BlockSpec((tm, d), lambda i, j: (i, 0)),
            pl.BlockSpec((1, d), lambda i, j: (0, 0)),
            pl.BlockSpec((d, tn), lambda i, j: (0, j)),
            pl.BlockSpec((tm, HEAD_DIM), lambda i, j: (i % pos_blocks, 0)),
            pl.BlockSpec((tm, HEAD_DIM), lambda i, j: (i % pos_blocks, 0)),
        ],
        out_specs=pl.BlockSpec((tm, tn), lambda i, j: (i, j)),
        out_shape=jax.ShapeDtypeStruct((t, n), BF16),
        scratch_shapes=[pltpu.VMEM((tm, d), BF16)],
        compiler_params=pltpu.CompilerParams(
            dimension_semantics=("arbitrary", "arbitrary"), vmem_limit_bytes=VMEM_LIMIT),
        name="inproj",
    )(x2d, g, w, cos_t, sin_t)


def _ret_kernel(q_ref, k_ref, v_ref, gate_ref, dmat_ref, qdec_ref, kdec_ref, sdec_ref, gn_ref,
                o_ref, *, n_heads, n_trips):
    hd = HEAD_DIM

    def step(r, hh, state):
        c = slice(hh * hd, (hh + 1) * hd)
        q = q_ref[pl.ds(r, RET_BLOCK), c]
        k = k_ref[pl.ds(r, RET_BLOCK), c]
        v = v_ref[pl.ds(r, RET_BLOCK), c]
        s = lax.dot_general(q, k, _NT, preferred_element_type=F32) * dmat_ref[hh]
        intra = jnp.dot(s.astype(BF16), v, preferred_element_type=F32)
        qd = (q.astype(F32) * qdec_ref[hh]).astype(BF16)
        cross = jnp.dot(qd, state.astype(BF16), preferred_element_type=F32)
        kd = (k.astype(F32) * kdec_ref[hh]).astype(BF16)
        kv = lax.dot_general(kd, v, _TN, preferred_element_type=F32)
        o = intra + cross
        mu = jnp.mean(o, axis=-1, keepdims=True)
        oc = o - mu
        var = jnp.mean(oc * oc, axis=-1, keepdims=True)
        y = oc * lax.rsqrt(var + EPS) * gn_ref[hh]
        gate = gate_ref[pl.ds(r, RET_BLOCK), c].astype(F32)
        o_ref[pl.ds(r, RET_BLOCK), c] = (y * (gate * _sigmoid(gate))).astype(BF16)
        return state * sdec_ref[hh] + kv

    def body(n, states):
        states = list(states)
        for u in range(RET_UNROLL):
            r = pl.multiple_of((n * RET_UNROLL + u) * RET_BLOCK, RET_BLOCK)
            for hh in range(n_heads):
                states[hh] = step(r, hh, states[hh])
        return tuple(states)

    lax.fori_loop(0, n_trips, body,
                  tuple(jnp.zeros((hd, hd), F32) for _ in range(n_heads)))


def _retention(proj, dmat, qdec, kdec, sdec, gn, *, n_heads, col_q, col_k, col_v, col_g):
    b, s, _ = proj.shape
    w = n_heads * HEAD_DIM
    assert s % (RET_BLOCK * RET_UNROLL) == 0
    assert col_q % w == 0 and col_k % w == 0 and col_v % w == 0 and col_g % w == 0

    def col(c0):
        return pl.BlockSpec((None, s, w), lambda bi: (bi, 0, c0 // w))

    def table(a):
        return pl.BlockSpec(a.shape, lambda bi: (0, 0, 0))

    return pl.pallas_call(
        functools.partial(_ret_kernel, n_heads=n_heads, n_trips=s // (RET_BLOCK * RET_UNROLL)),
        grid=(b,),
        in_specs=[col(col_q), col(col_k), col(col_v), col(col_g),
                  table(dmat), table(qdec), table(kdec), table(sdec), table(gn)],
        out_specs=pl.BlockSpec((None, s, w), lambda bi: (bi, 0, 0)),
        out_shape=jax.ShapeDtypeStruct((b, s, w), BF16),
        compiler_params=pltpu.CompilerParams(
            dimension_semantics=("arbitrary",), vmem_limit_bytes=VMEM_LIMIT),
        name="retention",
    )(proj, proj, proj, proj, dmat, qdec, kdec, sdec, gn)


def _sb_kernel(q_ref, k_ref, v_ref, tri_ref, gn_ref, o_ref, acc_ref, carry_ref, *, n_heads):
    qi = pl.program_id(1)
    t = SB_TILE
    hd = HEAD_DIM
    row = lax.broadcasted_iota(jnp.int32, (t, t), 0)
    colm = lax.broadcasted_iota(jnp.int32, (t, t), 1)
    valid = colm < row

    def tile(j, masked):
        r = pl.multiple_of(j * t, t)
        heads = [slice(hh * hd, (hh + 1) * hd) for hh in range(n_heads)]
        hilo, logb, rows = [], [], []
        for c in heads:
            z = lax.dot_general(q_ref[:, c], k_ref[pl.ds(r, t), c], _NT,
                                preferred_element_type=F32)
            sp = jnp.maximum(z, 0.0) + jnp.log2(1.0 + jnp.exp2(-jnp.abs(z)))
            spm = jnp.where(valid, sp, 0.0) if masked else sp
            hi = spm.astype(BF16)
            lo = (spm - hi.astype(F32)).astype(BF16)
            hilo.append(jnp.concatenate([hi, lo], axis=1))
            logb.append(z - sp)
            rows.append(jnp.sum(spm, axis=-1, keepdims=True))
        ws = []
        for hh in range(n_heads):
            after = jnp.dot(hilo[hh], tri_ref[...], preferred_element_type=F32)
            w = jnp.exp2(logb[hh] - after)
            if masked:
                w = jnp.where(valid, w, 0.0)
            ws.append(w.astype(BF16))
        for hh, c in enumerate(heads):
            pv = jnp.dot(ws[hh], v_ref[pl.ds(r, t), c], preferred_element_type=F32)
            if masked:
                acc_ref[:, c] = pv
                carry_ref[:, c] = jnp.broadcast_to(rows[hh], (t, hd))
            else:
                carry = carry_ref[:, c]
                acc_ref[:, c] += jnp.exp2(-carry) * pv
                carry_ref[:, c] = carry + rows[hh]

    tile(qi, True)

    def body(it, _):
        tile(qi - 1 - it, False)
        return 0

    lax.fori_loop(0, qi, body, 0)

    for hh in range(n_heads):
        c = slice(hh * hd, (hh + 1) * hd)
        acc = acc_ref[:, c]
        y = acc * lax.rsqrt(jnp.mean(acc * acc, axis=-1, keepdims=True) + EPS)
        o_ref[:, c] = (y * gn_ref[:, c]).astype(BF16)


def _stick_breaking(proj, tri, gn, *, n_heads, col_q, col_k, col_v):
    b, s, _ = proj.shape
    w = n_heads * HEAD_DIM
    t = SB_TILE
    assert s % t == 0 and col_q % w == 0 and col_k % w == 0 and col_v % w == 0
    return pl.pallas_call(
        functools.partial(_sb_kernel, n_heads=n_heads),
        grid=(b, s // t),
        in_specs=[
            pl.BlockSpec((None, t, w), lambda bi, qi: (bi, qi, col_q // w)),
            pl.BlockSpec((None, s, w), lambda bi, qi: (bi, 0, col_k // w)),
            pl.BlockSpec((None, s, w), lambda bi, qi: (bi, 0, col_v // w)),
            pl.BlockSpec(tri.shape, lambda bi, qi: (0, 0)),
            pl.BlockSpec((1, w), lambda bi, qi: (0, 0)),
        ],
        out_specs=pl.BlockSpec((None, t, w), lambda bi, qi: (bi, qi, 0)),
        out_shape=jax.ShapeDtypeStruct((b, s, w), BF16),
        scratch_shapes=[pltpu.VMEM((t, w), F32), pltpu.VMEM((t, w), F32)],
        compiler_params=pltpu.CompilerParams(
            dimension_semantics=("arbitrary", "arbitrary"), vmem_limit_bytes=VMEM_LIMIT),
        name="stick_breaking",
    )(proj, proj, proj, tri, gn)


def _mix_ffn_kernel(x_ref, ret_ref, sb_ref, wo_ref, g2_ref, wg_ref, wu_ref, wd_ref, fg_ref,
                    o_ref, acc_ref, h_ref, *, final_norm, tf):
    ka = ret_ref.shape[1]
    x1 = (x_ref[...]
          + jnp.dot(ret_ref[...], wo_ref[:ka, :], preferred_element_type=F32)
          + jnp.dot(sb_ref[...], wo_ref[ka:, :], preferred_element_type=F32))
    acc_ref[...] = x1
    y = x1 * lax.rsqrt(jnp.mean(x1 * x1, axis=-1, keepdims=True) + EPS)
    h_ref[...] = (y * g2_ref[...]).astype(BF16)
    def ff_tile(f, _):
        f0 = pl.multiple_of(f * tf, tf)
        gt = jnp.dot(h_ref[...], wg_ref[:, pl.ds(f0, tf)], preferred_element_type=F32)
        up = jnp.dot(h_ref[...], wu_ref[:, pl.ds(f0, tf)], preferred_element_type=F32)
        a = (gt * _sigmoid(gt) * up).astype(BF16)
        acc_ref[...] += jnp.dot(a, wd_ref[pl.ds(f0, tf), :], preferred_element_type=F32)
        return 0

    lax.fori_loop(0, wd_ref.shape[0] // tf, ff_tile, 0)
    x2 = acc_ref[...]
    if final_norm:
        y = x2 * lax.rsqrt(jnp.mean(x2 * x2, axis=-1, keepdims=True) + EPS)
        o_ref[...] = y * fg_ref[...]
    else:
        o_ref[...] = x2


def _mix_ffn(x2d, ret2d, sb2d, wo, g2, wg, wu, wd, fg, *, final_norm, tm=512, tf=FFN_TILE):
    t, d = x2d.shape
    ka, kb = ret2d.shape[1], sb2d.shape[1]
    ff = wg.shape[1]
    assert t % tm == 0 and ff % tf == 0 and wo.shape[0] == ka + kb

    def resident(a):
        return pl.BlockSpec(a.shape, lambda i: (0, 0), pipeline_mode=pl.Buffered(1))

    return pl.pallas_call(
        functools.partial(_mix_ffn_kernel, final_norm=final_norm, tf=tf),
        grid=(t // tm,),
        in_specs=[
            pl.BlockSpec((tm, d), lambda i: (i, 0)),
            pl.BlockSpec((tm, ka), lambda i: (i, 0)),
            pl.BlockSpec((tm, kb), lambda i: (i, 0)),
            resident(wo),
            pl.BlockSpec((1, d), lambda i: (0, 0)),
            resident(wg), resident(wu), resident(wd),
            pl.BlockSpec((1, d), lambda i: (0, 0)),
        ],
        out_specs=pl.BlockSpec((tm, d), lambda i: (i, 0)),
        out_shape=jax.ShapeDtypeStruct((t, d), F32),
        scratch_shapes=[pltpu.VMEM((tm, d), F32), pltpu.VMEM((tm, d), BF16)],
        compiler_params=pltpu.CompilerParams(
            dimension_semantics=("arbitrary",), vmem_limit_bytes=VMEM_LIMIT),
        name="mix_ffn",
    )(x2d, ret2d, sb2d, wo, g2, wg, wu, wd, fg)


def _rotary_tables(seq):
    d = HEAD_DIM
    inv_freq = 1.0 / (ROPE_BASE ** (jnp.arange(0, d, 2, dtype=F32) / d))
    ang = jnp.arange(seq, dtype=F32)[:, None] * inv_freq[None, :]
    cos = jnp.cos(ang)
    sin = jnp.sin(ang)
    return jnp.concatenate([cos, cos], axis=-1), jnp.concatenate([-sin, sin], axis=-1)


def _retention_tables(n_heads):
    c = RET_BLOCK
    scale = HEAD_DIM ** -0.5
    log_g = jnp.log1p(-jnp.exp2(-5.0 - jnp.arange(n_heads, dtype=F32)))
    i = jnp.arange(c, dtype=F32)
    diff = i[:, None] - i[None, :]
    ci = jnp.arange(c) // RET_CHUNK
    same = ci[:, None] == ci[None, :]
    earlier = ci[None, :] < ci[:, None]
    dist = jnp.where(same, jnp.abs(diff), diff)
    dmat = jnp.where((same | earlier)[None], jnp.exp(log_g[:, None, None] * dist[None]), 0.0) * scale
    ones = jnp.ones((1, 1, HEAD_DIM), F32)
    qdec = jnp.exp(log_g[:, None] * (i + 1.0))[..., None] * ones
    kdec = jnp.exp(log_g[:, None] * (c - 1.0 - i))[..., None] * scale * ones
    sdec = jnp.exp(log_g * c)[:, None, None] * ones
    return dmat.astype(F32), qdec, kdec, sdec


def _after_matrix():
    j = jnp.arange(SB_TILE)
    tri = (j[:, None] > j[None, :]).astype(BF16)
    return jnp.concatenate([tri, tri], axis=0)


def kernel(x, norm1_g, w_in, ret_norm_g, sb_norm_g, w_out, norm2_g, w_gate, w_up, w_down, final_g):
    b, s, d = x.shape
    depth = w_in.shape[0]
    ret_w = ret_norm_g.shape[1]
    sb_w = sb_norm_g.shape[1]
    n_ret, n_sb = ret_w // HEAD_DIM, sb_w // HEAD_DIM
    c_rq, c_rk, c_rv, c_rg = 0, ret_w, 2 * ret_w, 3 * ret_w
    c_sq, c_sk, c_sv = 4 * ret_w, 4 * ret_w + sb_w, 4 * ret_w + 2 * sb_w

    cos_t, sin_t = _rotary_tables(s)
    dmat, qdec, kdec, sdec = _retention_tables(n_ret)
    tri = _after_matrix()

    xc = x.reshape(b * s, d)
    for l in range(depth):
        proj = _inproj(xc, norm1_g[l][None], w_in[l].astype(BF16), cos_t, sin_t,
                       seq=s, n_rot_cols=2 * ret_w, scaled_col=c_sq)
        proj = proj.reshape(b, s, -1)
        ret = _retention(proj, dmat, qdec, kdec, sdec, ret_norm_g[l].reshape(n_ret, 1, HEAD_DIM),
                         n_heads=n_ret, col_q=c_rq, col_k=c_rk, col_v=c_rv, col_g=c_rg)
        sb = _stick_breaking(proj, tri, sb_norm_g[l][None],
                             n_heads=n_sb, col_q=c_sq, col_k=c_sk, col_v=c_sv)
        xc = _mix_ffn(xc, ret.reshape(b * s, ret_w), sb.reshape(b * s, sb_w),
                      w_out[l].astype(BF16), norm2_g[l][None], w_gate[l].astype(BF16),
                      w_up[l].astype(BF16), w_down[l].astype(BF16), final_g[None],
                      final_norm=(l == depth - 1))
    return xc.reshape(b, s, d)
```

```python
import functools

import jax
import jax.numpy as jnp
from jax import lax
from jax.experimental import pallas as pl
from jax.experimental.pallas import tpu as pltpu

HEAD_DIM = 128
ROPE_BASE = 10000.0
EPS = 1e-6
LOG2_E = 1.4426950408889634
RET_BLOCK = 128
RET_CHUNK = 64
RET_UNROLL = 2
SB_TILE = 256
FFN_TILE = 256
SB_DEAD_LOG2 = 151.0
VMEM_LIMIT = 48 * 1024 * 1024

F32 = jnp.float32
BF16 = jnp.bfloat16
_NT = (((1,), (1,)), ((), ()))
_TN = (((0,), (0,)), ((), ()))


def _sigmoid(x):
    return 1.0 / (1.0 + jnp.exp(-x))


def _inproj_kernel(x_ref, g_ref, wb_ref, cos_ref, sin_ref, o_ref, *,
                   n_rot_cols, scaled_cols, scale, chunk):
    xf = x_ref[...]
    y = xf * lax.rsqrt(jnp.mean(xf * xf, axis=-1, keepdims=True) + EPS)
    h = (y * g_ref[...]).astype(BF16)
    cos = cos_ref[...]
    sin = sin_ref[...]
    for c0 in range(0, wb_ref.shape[1], chunk):
        acc = jnp.dot(h, wb_ref[:, c0:c0 + chunk], preferred_element_type=F32)
        for h0 in range(0, chunk, HEAD_DIM):
            a = acc[:, h0:h0 + HEAD_DIM]
            col = c0 + h0
            if col < n_rot_cols:
                a = a * cos + pltpu.roll(a, HEAD_DIM // 2, 1) * sin
            elif scaled_cols[0] <= col < scaled_cols[1]:
                a = a * scale
            o_ref[:, col:col + HEAD_DIM] = a.astype(BF16)


def _inproj(x2d, g, w, cos_t, sin_t, *, seq, n_rot_cols, scaled_cols,
            tm=512, chunk=512):
    t, d = x2d.shape
    n = w.shape[1]
    assert t % tm == 0 and seq % tm == 0 and n % chunk == 0 and chunk % HEAD_DIM == 0
    pos_blocks = seq // tm
    return pl.pallas_call(
        functools.partial(_inproj_kernel, n_rot_cols=n_rot_cols, scaled_cols=scaled_cols,
                          scale=HEAD_DIM ** -0.5 * LOG2_E, chunk=chunk),
        grid=(t // tm,),
        in_specs=[
            pl.BlockSpec((tm, d), lambda i: (i, 0)),
            pl.BlockSpec((1, d), lambda i: (0, 0)),
            pl.BlockSpec((d, n), lambda i: (0, 0), pipeline_mode=pl.Buffered(1)),
            pl.BlockSpec((tm, HEAD_DIM), lambda i: (i % pos_blocks, 0)),
            pl.BlockSpec((tm, HEAD_DIM), lambda i: (i % pos_blocks, 0)),
        ],
        out_specs=pl.BlockSpec((tm, n), lambda i: (i, 0)),
        out_shape=jax.ShapeDtypeStruct((t, n), BF16),
        compiler_params=pltpu.CompilerParams(
            dimension_semantics=("arbitrary",), vmem_limit_bytes=VMEM_LIMIT),
        name="inproj",
    )(x2d, g, w, cos_t, sin_t)


def _ret_kernel(q_ref, k_ref, v_ref, gate_ref, dmat_ref, qdec_ref, kdec_ref, sdec_ref, gn_ref,
                o_ref, *, n_heads, n_trips):
    hd = HEAD_DIM

    def step(r, hh, state):
        c = slice(hh * hd, (hh + 1) * hd)
        q = q_ref[pl.ds(r, RET_BLOCK), c]
        k = k_ref[pl.ds(r, RET_BLOCK), c]
        v = v_ref[pl.ds(r, RET_BLOCK), c]
        s = lax.dot_general(q, k, _NT, preferred_element_type=F32) * dmat_ref[hh]
        intra = jnp.dot(s.astype(BF16), v, preferred_element_type=F32)
        qd = (q.astype(F32) * qdec_ref[hh]).astype(BF16)
        cross = jnp.dot(qd, state.astype(BF16), preferred_element_type=F32)
        kd = (k.astype(F32) * kdec_ref[hh]).astype(BF16)
        kv = lax.dot_general(kd, v, _TN, preferred_element_type=F32)
        o = intra + cross
        mu = jnp.mean(o, axis=-1, keepdims=True)
        oc = o - mu
        var = jnp.mean(oc * oc, axis=-1, keepdims=True)
        y = oc * lax.rsqrt(var + EPS) * gn_ref[hh]
        gate = gate_ref[pl.ds(r, RET_BLOCK), c].astype(F32)
        o_ref[pl.ds(r, RET_BLOCK), c] = (y * (gate * _sigmoid(gate))).astype(BF16)
        return state * sdec_ref[hh] + kv

    def body(n, states):
        states = list(states)
        for u in range(RET_UNROLL):
            r = pl.multiple_of((n * RET_UNROLL + u) * RET_BLOCK, RET_BLOCK)
            for hh in range(n_heads):
                states[hh] = step(r, hh, states[hh])
        return tuple(states)

    lax.fori_loop(0, n_trips, body,
                  tuple(jnp.zeros((hd, hd), F32) for _ in range(n_heads)))


def _retention(proj, dmat, qdec, kdec, sdec, gn, *, n_heads, col_q, col_k, col_v, col_g):
    b, s, _ = proj.shape
    w = n_heads * HEAD_DIM
    assert s % (RET_BLOCK * RET_UNROLL) == 0
    assert col_q % w == 0 and col_k % w == 0 and col_v % w == 0 and col_g % w == 0

    def col(c0):
        return pl.BlockSpec((None, s, w), lambda bi: (bi, 0, c0 // w))

    def table(a):
        return pl.BlockSpec(a.shape, lambda bi: (0, 0, 0))

    return pl.pallas_call(
        functools.partial(_ret_kernel, n_heads=n_heads, n_trips=s // (RET_BLOCK * RET_UNROLL)),
        grid=(b,),
        in_specs=[col(col_q), col(col_k), col(col_v), col(col_g),
                  table(dmat), table(qdec), table(kdec), table(sdec), table(gn)],
        out_specs=pl.BlockSpec((None, s, w), lambda bi: (bi, 0, 0)),
        out_shape=jax.ShapeDtypeStruct((b, s, w), BF16),
        compiler_params=pltpu.CompilerParams(
            dimension_semantics=("arbitrary",), vmem_limit_bytes=VMEM_LIMIT),
        name="retention",
    )(proj, proj, proj, proj, dmat, qdec, kdec, sdec, gn)


def _sb_kernel(q_ref, k_ref, v_ref, tri_ref, gn_ref, o_ref, acc_ref, carry_ref, *, n_heads):
    qi = pl.program_id(1)
    t = SB_TILE
    hd = HEAD_DIM
    row = lax.broadcasted_iota(jnp.int32, (t, t), 0)
    colm = lax.broadcasted_iota(jnp.int32, (t, t), 1)
    valid = colm < row

    def tile(j, masked):
        r = pl.multiple_of(j * t, t)
        heads = [slice(hh * hd, (hh + 1) * hd) for hh in range(n_heads)]
        hilo, logb, rows = [], [], []
        for c in heads:
            z = lax.dot_general(q_ref[:, c], k_ref[pl.ds(r, t), c], _NT,
                                preferred_element_type=F32)
            sp = jnp.maximum(z, 0.0) + jnp.log2(1.0 + jnp.exp2(-jnp.abs(z)))
            spm = jnp.where(valid, sp, 0.0) if masked else sp
            hi = spm.astype(BF16)
            lo = (spm - hi.astype(F32)).astype(BF16)
            hilo.append(jnp.concatenate([hi, lo], axis=1))
            logb.append(z - sp)
            rows.append(jnp.sum(spm, axis=-1, keepdims=True))
        ws = []
        for hh in range(n_heads):
            after = jnp.dot(hilo[hh], tri_ref[...], preferred_element_type=F32)
            w = jnp.exp2(logb[hh] - after)
            if masked:
                w = jnp.where(valid, w, 0.0)
            ws.append(w.astype(BF16))
        for hh, c in enumerate(heads):
            pv = jnp.dot(ws[hh], v_ref[pl.ds(r, t), c], preferred_element_type=F32)
            if masked:
                acc_ref[:, c] = pv
                carry_ref[:, c] = jnp.broadcast_to(rows[hh], (t, hd))
            else:
                carry = carry_ref[:, c]
                acc_ref[:, c] += jnp.exp2(-carry) * pv
                carry_ref[:, c] = carry + rows[hh]

    tile(qi, True)

    def more(state):
        it, least = state
        return (it < qi) & (least < SB_DEAD_LOG2)

    def body(state):
        it, _ = state
        tile(qi - 1 - it, False)
        return it + 1, jnp.min(carry_ref[...])

    lax.while_loop(more, body, (jnp.int32(0), jnp.min(carry_ref[...])))

    for hh in range(n_heads):
        c = slice(hh * hd, (hh + 1) * hd)
        acc = acc_ref[:, c]
        y = acc * lax.rsqrt(jnp.mean(acc * acc, axis=-1, keepdims=True) + EPS)
        o_ref[:, c] = (y * gn_ref[:, c]).astype(BF16)


def _stick_breaking(proj, tri, gn, *, n_heads, col_q, col_k, col_v):
    b, s, _ = proj.shape
    w = n_heads * HEAD_DIM
    t = SB_TILE
    assert s % t == 0 and col_q % w == 0 and col_k % w == 0 and col_v % w == 0
    return pl.pallas_call(
        functools.partial(_sb_kernel, n_heads=n_heads),
        grid=(b, s // t),
        in_specs=[
            pl.BlockSpec((None, t, w), lambda bi, qi: (bi, qi, col_q // w)),
            pl.BlockSpec((None, s, w), lambda bi, qi: (bi, 0, col_k // w)),
            pl.BlockSpec((None, s, w), lambda bi, qi: (bi, 0, col_v // w)),
            pl.BlockSpec(tri.shape, lambda bi, qi: (0, 0)),
            pl.BlockSpec((1, w), lambda bi, qi: (0, 0)),
        ],
        out_specs=pl.BlockSpec((None, t, w), lambda bi, qi: (bi, qi, 0)),
        out_shape=jax.ShapeDtypeStruct((b, s, w), BF16),
        scratch_shapes=[pltpu.VMEM((t, w), F32), pltpu.VMEM((t, w), F32)],
        compiler_params=pltpu.CompilerParams(
            dimension_semantics=("arbitrary", "arbitrary"), vmem_limit_bytes=VMEM_LIMIT),
        name="stick_breaking",
    )(proj, proj, proj, tri, gn)


def _mix_ffn_kernel(x_ref, ret_ref, sb_ref, wo_ref, g2_ref, wg_ref, wu_ref, wd_ref, fg_ref,
                    o_ref, acc_ref, h_ref, *, final_norm, tf):
    ka = ret_ref.shape[1]
    x1 = (x_ref[...]
          + jnp.dot(ret_ref[...], wo_ref[:ka, :], preferred_element_type=F32)
          + jnp.dot(sb_ref[...], wo_ref[ka:, :], preferred_element_type=F32))
    acc_ref[...] = x1
    y = x1 * lax.rsqrt(jnp.mean(x1 * x1, axis=-1, keepdims=True) + EPS)
    h_ref[...] = (y * g2_ref[...]).astype(BF16)
    def ff_tile(f, _):
        f0 = pl.multiple_of(f * tf, tf)
        gt = jnp.dot(h_ref[...], wg_ref[:, pl.ds(f0, tf)], preferred_element_type=F32)
        up = jnp.dot(h_ref[...], wu_ref[:, pl.ds(f0, tf)], preferred_element_type=F32)
        a = (gt * _sigmoid(gt) * up).astype(BF16)
        acc_ref[...] += jnp.dot(a, wd_ref[pl.ds(f0, tf), :], preferred_element_type=F32)
        return 0

    lax.fori_loop(0, wd_ref.shape[0] // tf, ff_tile, 0)
    x2 = acc_ref[...]
    if final_norm:
        y = x2 * lax.rsqrt(jnp.mean(x2 * x2, axis=-1, keepdims=True) + EPS)
        o_ref[...] = y * fg_ref[...]
    else:
        o_ref[...] = x2


def _mix_ffn(x2d, ret2d, sb2d, wo, g2, wg, wu, wd, fg, *, final_norm, tm=512, tf=FFN_TILE):
    t, d = x2d.shape
    ka, kb = ret2d.shape[1], sb2d.shape[1]
    ff = wg.shape[1]
    assert t % tm == 0 and ff % tf == 0 and wo.shape[0] == ka + kb

    def resident(a):
        return pl.BlockSpec(a.shape, lambda i: (0, 0), pipeline_mode=pl.Buffered(1))

    return pl.pallas_call(
        functools.partial(_mix_ffn_kernel, final_norm=final_norm, tf=tf),
        grid=(t // tm,),
        in_specs=[
            pl.BlockSpec((tm, d), lambda i: (i, 0)),
            pl.BlockSpec((tm, ka), lambda i: (i, 0)),
            pl.BlockSpec((tm, kb), lambda i: (i, 0)),
            resident(wo),
            pl.BlockSpec((1, d), lambda i: (0, 0)),
            resident(wg), resident(wu), resident(wd),
            pl.BlockSpec((1, d), lambda i: (0, 0)),
        ],
        out_specs=pl.BlockSpec((tm, d), lambda i: (i, 0)),
        out_shape=jax.ShapeDtypeStruct((t, d), F32),
        scratch_shapes=[pltpu.VMEM((tm, d), F32), pltpu.VMEM((tm, d), BF16)],
        compiler_params=pltpu.CompilerParams(
            dimension_semantics=("arbitrary",), vmem_limit_bytes=VMEM_LIMIT),
        name="mix_ffn",
    )(x2d, ret2d, sb2d, wo, g2, wg, wu, wd, fg)


def _rotary_tables(seq):
    d = HEAD_DIM
    inv_freq = 1.0 / (ROPE_BASE ** (jnp.arange(0, d, 2, dtype=F32) / d))
    ang = jnp.arange(seq, dtype=F32)[:, None] * inv_freq[None, :]
    cos = jnp.cos(ang)
    sin = jnp.sin(ang)
    return jnp.concatenate([cos, cos], axis=-1), jnp.concatenate([-sin, sin], axis=-1)


def _retention_tables(n_heads):
    c = RET_BLOCK
    scale = HEAD_DIM ** -0.5
    log_g = jnp.log1p(-jnp.exp2(-5.0 - jnp.arange(n_heads, dtype=F32)))
    i = jnp.arange(c, dtype=F32)
    diff = i[:, None] - i[None, :]
    ci = jnp.arange(c) // RET_CHUNK
    same = ci[:, None] == ci[None, :]
    earlier = ci[None, :] < ci[:, None]
    dist = jnp.where(same, jnp.abs(diff), diff)
    dmat = jnp.where((same | earlier)[None], jnp.exp(log_g[:, None, None] * dist[None]), 0.0) * scale
    ones = jnp.ones((1, 1, HEAD_DIM), F32)
    qdec = jnp.exp(log_g[:, None] * (i + 1.0))[..., None] * ones
    kdec = jnp.exp(log_g[:, None] * (c - 1.0 - i))[..., None] * scale * ones
    sdec = jnp.exp(log_g * c)[:, None, None] * ones
    return dmat.astype(F32), qdec, kdec, sdec


def _after_matrix():
    j = jnp.arange(SB_TILE)
    tri = (j[:, None] > j[None, :]).astype(BF16)
    return jnp.concatenate([tri, tri], axis=0)


def kernel(x, norm1_g, w_in, ret_norm_g, sb_norm_g, w_out, norm2_g, w_gate, w_up, w_down, final_g):
    b, s, d = x.shape
    depth = w_in.shape[0]
    ret_w = ret_norm_g.shape[1]
    sb_w = sb_norm_g.shape[1]
    n_ret, n_sb = ret_w // HEAD_DIM, sb_w // HEAD_DIM
    c_rq, c_rk, c_rv, c_rg = 0, ret_w, 2 * ret_w, 3 * ret_w
    c_sq, c_sk, c_sv = 4 * ret_w, 4 * ret_w + sb_w, 4 * ret_w + 2 * sb_w

    cos_t, sin_t = _rotary_tables(s)
    dmat, qdec, kdec, sdec = _retention_tables(n_ret)
    tri = _after_matrix()

    xc = x.reshape(b * s, d)
    for l in range(depth):
        proj = _inproj(xc, norm1_g[l][None], w_in[l].astype(BF16), cos_t, sin_t,
                       seq=s, n_rot_cols=2 * ret_w, scaled_cols=(c_sq, c_sk))
        proj = proj.reshape(b, s, -1)
        ret = _retention(proj, dmat, qdec, kdec, sdec, ret_norm_g[l].reshape(n_ret, 1, HEAD_DIM),
                         n_heads=n_ret, col_q=c_rq, col_k=c_rk, col_v=c_rv, col_g=c_rg)
        sb = _stick_breaking(proj, tri, sb_norm_g[l][None],
                             n_heads=n_sb, col_q=c_sq, col_k=c_sk, col_v=c_sv)
        xc = _mix_ffn(xc, ret.reshape(b * s, ret_w), sb.reshape(b * s, sb_w),
                      w_out[l].astype(BF16), norm2_g[l][None], w_gate[l].astype(BF16),
                      w_up[l].astype(BF16), w_down[l].astype(BF16), final_g[None],
                      final_norm=(l == depth - 1))
    return xc.reshape(b, s, d)
```

```python
import functools

import jax
import jax.numpy as jnp
from jax import lax
from jax.experimental import pallas as pl
from jax.experimental.pallas import tpu as pltpu

HEAD_DIM = 128
ROPE_BASE = 10000.0
EPS = 1e-6
LOG2_E = 1.4426950408889634
RET_BLOCK = 128
RET_CHUNK = 64
RET_UNROLL = 4
SB_TILE = 256
FFN_TILE = 512
MIX_ROWS = 256
SB_DEAD_LOG2 = 151.0
VMEM_LIMIT = 48 * 1024 * 1024

F32 = jnp.float32
BF16 = jnp.bfloat16
_NT = (((1,), (1,)), ((), ()))
_TN = (((0,), (0,)), ((), ()))


def _sigmoid(x):
    return 1.0 / (1.0 + jnp.exp(-x))


def _inproj_kernel(x_ref, g_ref, w_ref, cos_ref, sin_ref, o_ref, wb_ref, *,
                   n_rot_cols, scaled_cols, scale, chunk):
    @pl.when(pl.program_id(0) == 0)
    def _():
        for c0 in range(0, wb_ref.shape[1], chunk):
            wb_ref[:, c0:c0 + chunk] = w_ref[:, c0:c0 + chunk].astype(BF16)

    xf = x_ref[...]
    y = xf * lax.rsqrt(jnp.mean(xf * xf, axis=-1, keepdims=True) + EPS)
    h = (y * g_ref[...]).astype(BF16)
    cos = cos_ref[...]
    sin = sin_ref[...]
    for c0 in range(0, wb_ref.shape[1], chunk):
        acc = jnp.dot(h, wb_ref[:, c0:c0 + chunk], preferred_element_type=F32)
        for h0 in range(0, chunk, HEAD_DIM):
            a = acc[:, h0:h0 + HEAD_DIM]
            col = c0 + h0
            if col < n_rot_cols:
                a = a * cos + pltpu.roll(a, HEAD_DIM // 2, 1) * sin
            elif scaled_cols[0] <= col < scaled_cols[1]:
                a = a * scale
            o_ref[:, col:col + HEAD_DIM] = a.astype(BF16)


def _inproj(x2d, g, w_all, layer, cos_t, sin_t, *, seq, n_rot_cols, scaled_cols,
            tm=512, chunk=512):
    t, d = x2d.shape
    n = w_all.shape[2]
    assert t % tm == 0 and seq % tm == 0 and n % chunk == 0 and chunk % HEAD_DIM == 0
    pos_blocks = seq // tm
    return pl.pallas_call(
        functools.partial(_inproj_kernel, n_rot_cols=n_rot_cols, scaled_cols=scaled_cols,
                          scale=HEAD_DIM ** -0.5 * LOG2_E, chunk=chunk),
        grid=(t // tm,),
        in_specs=[
            pl.BlockSpec((tm, d), lambda i: (i, 0)),
            pl.BlockSpec((1, d), lambda i: (0, 0)),
            pl.BlockSpec((None, d, n), lambda i: (layer, 0, 0), pipeline_mode=pl.Buffered(1)),
            pl.BlockSpec((tm, HEAD_DIM), lambda i: (i % pos_blocks, 0)),
            pl.BlockSpec((tm, HEAD_DIM), lambda i: (i % pos_blocks, 0)),
        ],
        out_specs=pl.BlockSpec((tm, n), lambda i: (i, 0)),
        out_shape=jax.ShapeDtypeStruct((t, n), BF16),
        scratch_shapes=[pltpu.VMEM((d, n), BF16)],
        compiler_params=pltpu.CompilerParams(
            dimension_semantics=("arbitrary",), vmem_limit_bytes=VMEM_LIMIT),
        name="inproj",
    )(x2d, g, w_all, cos_t, sin_t)


def _ret_kernel(q_ref, k_ref, v_ref, gate_ref, dmat_ref, qdec_ref, kdec_ref, sdec_ref, gn_ref,
                *rest, n_heads, n_trips, n_cast):
    hd = HEAD_DIM
    o_ref = rest[n_cast]
    for src_ref, dst_ref in zip(rest[:n_cast], rest[n_cast + 1:]):
        dst_ref[...] = src_ref[...].astype(BF16)

    def step(r, hh, state):
        c = slice(hh * hd, (hh + 1) * hd)
        q = q_ref[pl.ds(r, RET_BLOCK), c]
        k = k_ref[pl.ds(r, RET_BLOCK), c]
        v = v_ref[pl.ds(r, RET_BLOCK), c]
        s = lax.dot_general(q, k, _NT, preferred_element_type=F32) * dmat_ref[hh]
        intra = jnp.dot(s.astype(BF16), v, preferred_element_type=F32)
        qd = (q.astype(F32) * qdec_ref[hh]).astype(BF16)
        cross = jnp.dot(qd, state.astype(BF16), preferred_element_type=F32)
        kd = (k.astype(F32) * kdec_ref[hh]).astype(BF16)
        kv = lax.dot_general(kd, v, _TN, preferred_element_type=F32)
        o = intra + cross
        mu = jnp.mean(o, axis=-1, keepdims=True)
        oc = o - mu
        var = jnp.mean(oc * oc, axis=-1, keepdims=True)
        y = oc * lax.rsqrt(var + EPS) * gn_ref[hh]
        gate = gate_ref[pl.ds(r, RET_BLOCK), c].astype(F32)
        o_ref[pl.ds(r, RET_BLOCK), c] = (y * (gate * _sigmoid(gate))).astype(BF16)
        return state * sdec_ref[hh] + kv

    def body(n, states):
        states = list(states)
        for u in range(RET_UNROLL):
            r = pl.multiple_of((n * RET_UNROLL + u) * RET_BLOCK, RET_BLOCK)
            for hh in range(n_heads):
                states[hh] = step(r, hh, states[hh])
        return tuple(states)

    lax.fori_loop(0, n_trips, body,
                  tuple(jnp.zeros((hd, hd), F32) for _ in range(n_heads)))


def _retention(proj, dmat, qdec, kdec, sdec, gn, cast_weights, layer, *,
               n_heads, col_q, col_k, col_v, col_g):
    b, s, _ = proj.shape
    w = n_heads * HEAD_DIM
    assert s % (RET_BLOCK * RET_UNROLL) == 0
    assert col_q % w == 0 and col_k % w == 0 and col_v % w == 0 and col_g % w == 0
    assert all(a.shape[1] % (b * 16) == 0 for a in cast_weights)

    def col(c0):
        return pl.BlockSpec((None, s, w), lambda bi: (bi, 0, c0 // w))

    def table(a):
        return pl.BlockSpec(a.shape, lambda bi: (0, 0, 0))

    def slab_in(a):
        return pl.BlockSpec((None, a.shape[1] // b, a.shape[2]), lambda bi: (layer, bi, 0))

    def slab_out(a):
        return pl.BlockSpec((a.shape[1] // b, a.shape[2]), lambda bi: (bi, 0))

    outs = pl.pallas_call(
        functools.partial(_ret_kernel, n_heads=n_heads, n_trips=s // (RET_BLOCK * RET_UNROLL),
                          n_cast=len(cast_weights)),
        grid=(b,),
        in_specs=[col(col_q), col(col_k), col(col_v), col(col_g),
                  table(dmat), table(qdec), table(kdec), table(sdec), table(gn)]
                 + [slab_in(a) for a in cast_weights],
        out_specs=[pl.BlockSpec((None, s, w), lambda bi: (bi, 0, 0))]
                  + [slab_out(a) for a in cast_weights],
        out_shape=[jax.ShapeDtypeStruct((b, s, w), BF16)]
                  + [jax.ShapeDtypeStruct(a.shape[1:], BF16) for a in cast_weights],
        compiler_params=pltpu.CompilerParams(
            dimension_semantics=("arbitrary",), vmem_limit_bytes=VMEM_LIMIT),
        name="retention",
    )(proj, proj, proj, proj, dmat, qdec, kdec, sdec, gn, *cast_weights)
    return outs[0], outs[1:]


def _sb_kernel(q_ref, k_ref, v_ref, tri_ref, gn_ref, o_ref, acc_ref, carry_ref, *, n_heads):
    qi = pl.program_id(1)
    t = SB_TILE
    hd = HEAD_DIM
    row = lax.broadcasted_iota(jnp.int32, (t, t), 0)
    colm = lax.broadcasted_iota(jnp.int32, (t, t), 1)
    valid = colm < row

    def tile(j, masked):
        r = pl.multiple_of(j * t, t)
        heads = [slice(hh * hd, (hh + 1) * hd) for hh in range(n_heads)]
        hilo, logb, rows = [], [], []
        for c in heads:
            z = lax.dot_general(q_ref[:, c], k_ref[pl.ds(r, t), c], _NT,
                                preferred_element_type=F32)
            sp = jnp.maximum(z, 0.0) + jnp.log2(1.0 + jnp.exp2(-jnp.abs(z)))
            spm = jnp.where(valid, sp, 0.0) if masked else sp
            hi = spm.astype(BF16)
            lo = (spm - hi.astype(F32)).astype(BF16)
            hilo.append(jnp.concatenate([hi, lo], axis=1))
            logb.append(z - sp)
            rows.append(jnp.sum(spm, axis=-1, keepdims=True))
        ws = []
        for hh in range(n_heads):
            after = jnp.dot(hilo[hh], tri_ref[...], preferred_element_type=F32)
            w = jnp.exp2(logb[hh] - after)
            if masked:
                w = jnp.where(valid, w, 0.0)
            ws.append(w.astype(BF16))
        for hh, c in enumerate(heads):
            pv = jnp.dot(ws[hh], v_ref[pl.ds(r, t), c], preferred_element_type=F32)
            if masked:
                acc_ref[:, c] = pv
                carry_ref[:, c] = jnp.broadcast_to(rows[hh], (t, hd))
            else:
                carry = carry_ref[:, c]
                acc_ref[:, c] += jnp.exp2(-carry) * pv
                carry_ref[:, c] = carry + rows[hh]

    tile(qi, True)

    def more(state):
        it, least = state
        return (it < qi) & (least < SB_DEAD_LOG2)

    def body(state):
        it, _ = state
        tile(qi - 1 - it, False)
        return it + 1, jnp.min(carry_ref[...])

    lax.while_loop(more, body, (jnp.int32(0), jnp.min(carry_ref[...])))

    for hh in range(n_heads):
        c = slice(hh * hd, (hh + 1) * hd)
        acc = acc_ref[:, c]
        y = acc * lax.rsqrt(jnp.mean(acc * acc, axis=-1, keepdims=True) + EPS)
        o_ref[:, c] = (y * gn_ref[:, c]).astype(BF16)


def _stick_breaking(proj, tri, gn, *, n_heads, col_q, col_k, col_v):
    b, s, _ = proj.shape
    w = n_heads * HEAD_DIM
    t = SB_TILE
    assert s % t == 0 and col_q % w == 0 and col_k % w == 0 and col_v % w == 0
    return pl.pallas_call(
        functools.partial(_sb_kernel, n_heads=n_heads),
        grid=(b, s // t),
        in_specs=[
            pl.BlockSpec((None, t, w), lambda bi, qi: (bi, qi, col_q // w)),
            pl.BlockSpec((None, s, w), lambda bi, qi: (bi, 0, col_k // w)),
            pl.BlockSpec((None, s, w), lambda bi, qi: (bi, 0, col_v // w)),
            pl.BlockSpec(tri.shape, lambda bi, qi: (0, 0)),
            pl.BlockSpec((1, w), lambda bi, qi: (0, 0)),
        ],
        out_specs=pl.BlockSpec((None, t, w), lambda bi, qi: (bi, qi, 0)),
        out_shape=jax.ShapeDtypeStruct((b, s, w), BF16),
        scratch_shapes=[pltpu.VMEM((t, w), F32), pltpu.VMEM((t, w), F32)],
        compiler_params=pltpu.CompilerParams(
            dimension_semantics=("arbitrary", "arbitrary"), vmem_limit_bytes=VMEM_LIMIT),
        name="stick_breaking",
    )(proj, proj, proj, tri, gn)


def _mix_ffn_kernel(x_ref, ret_ref, sb_ref, wo_ref, g2_ref, wg_ref, wu_ref, wd_ref, fg_ref,
                    o_ref, acc_ref, h_ref, *, final_norm, tf):
    ka = ret_ref.shape[1]
    tm = x_ref.shape[0]
    ff = wd_ref.shape[0]
    for r0 in range(0, tm, MIX_ROWS):
        rows = slice(r0, r0 + MIX_ROWS)
        x1 = (x_ref[rows, :]
              + jnp.dot(ret_ref[rows, :], wo_ref[:ka, :], preferred_element_type=F32)
              + jnp.dot(sb_ref[rows, :], wo_ref[ka:, :], preferred_element_type=F32))
        acc_ref[rows, :] = x1
        y = x1 * lax.rsqrt(jnp.mean(x1 * x1, axis=-1, keepdims=True) + EPS)
        h_ref[rows, :] = (y * g2_ref[...]).astype(BF16)

    def ff_tile(f0, width):
        gt = jnp.dot(h_ref[...], wg_ref[:, pl.ds(f0, width)], preferred_element_type=F32)
        up = jnp.dot(h_ref[...], wu_ref[:, pl.ds(f0, width)], preferred_element_type=F32)
        a = (gt * _sigmoid(gt) * up).astype(BF16)
        acc_ref[...] += jnp.dot(a, wd_ref[pl.ds(f0, width), :], preferred_element_type=F32)

    def ff_body(f, _):
        ff_tile(pl.multiple_of(f * tf, tf), tf)
        return 0

    n_full = ff // tf
    lax.fori_loop(0, n_full, ff_body, 0)
    if ff % tf:
        ff_tile(n_full * tf, ff % tf)
    x2 = acc_ref[...]
    if final_norm:
        y = x2 * lax.rsqrt(jnp.mean(x2 * x2, axis=-1, keepdims=True) + EPS)
        o_ref[...] = y * fg_ref[...]
    else:
        o_ref[...] = x2


def _mix_ffn(x2d, ret2d, sb2d, wo, g2, wg, wu, wd, fg, *, final_norm, tm=512, tf=FFN_TILE):
    t, d = x2d.shape
    ka, kb = ret2d.shape[1], sb2d.shape[1]
    ff = wg.shape[1]
    assert t % tm == 0 and tm % MIX_ROWS == 0 and wo.shape[0] == ka + kb
    assert tf % 256 == 0 and (ff % tf) % 256 == 0

    def resident(a):
        return pl.BlockSpec(a.shape, lambda i: (0, 0), pipeline_mode=pl.Buffered(1))

    return pl.pallas_call(
        functools.partial(_mix_ffn_kernel, final_norm=final_norm, tf=tf),
        grid=(t // tm,),
        in_specs=[
            pl.BlockSpec((tm, d), lambda i: (i, 0)),
            pl.BlockSpec((tm, ka), lambda i: (i, 0)),
            pl.BlockSpec((tm, kb), lambda i: (i, 0)),
            resident(wo),
            pl.BlockSpec((1, d), lambda i: (0, 0)),
            resident(wg), resident(wu), resident(wd),
            pl.BlockSpec((1, d), lambda i: (0, 0)),
        ],
        out_specs=pl.BlockSpec((tm, d), lambda i: (i, 0)),
        out_shape=jax.ShapeDtypeStruct((t, d), F32),
        scratch_shapes=[pltpu.VMEM((tm, d), F32), pltpu.VMEM((tm, d), BF16)],
        compiler_params=pltpu.CompilerParams(
            dimension_semantics=("arbitrary",), vmem_limit_bytes=VMEM_LIMIT),
        name="mix_ffn",
    )(x2d, ret2d, sb2d, wo, g2, wg, wu, wd, fg)


def _rotary_tables(seq):
    d = HEAD_DIM
    inv_freq = 1.0 / (ROPE_BASE ** (jnp.arange(0, d, 2, dtype=F32) / d))
    ang = jnp.arange(seq, dtype=F32)[:, None] * inv_freq[None, :]
    cos = jnp.cos(ang)
    sin = jnp.sin(ang)
    return jnp.concatenate([cos, cos], axis=-1), jnp.concatenate([-sin, sin], axis=-1)


def _retention_tables(n_heads):
    c = RET_BLOCK
    scale = HEAD_DIM ** -0.5
    log_g = jnp.log1p(-jnp.exp2(-5.0 - jnp.arange(n_heads, dtype=F32)))
    i = jnp.arange(c, dtype=F32)
    diff = i[:, None] - i[None, :]
    ci = jnp.arange(c) // RET_CHUNK
    same = ci[:, None] == ci[None, :]
    earlier = ci[None, :] < ci[:, None]
    dist = jnp.where(same, jnp.abs(diff), diff)
    dmat = jnp.where((same | earlier)[None], jnp.exp(log_g[:, None, None] * dist[None]), 0.0) * scale
    ones = jnp.ones((1, 1, HEAD_DIM), F32)
    qdec = jnp.exp(log_g[:, None] * (i + 1.0))[..., None] * ones
    kdec = jnp.exp(log_g[:, None] * (c - 1.0 - i))[..., None] * scale * ones
    sdec = jnp.exp(log_g * c)[:, None, None] * ones
    return dmat.astype(F32), qdec, kdec, sdec


def _after_matrix():
    j = jnp.arange(SB_TILE)
    tri = (j[:, None] > j[None, :]).astype(BF16)
    return jnp.concatenate([tri, tri], axis=0)


def kernel(x, norm1_g, w_in, ret_norm_g, sb_norm_g, w_out, norm2_g, w_gate, w_up, w_down, final_g):
    b, s, d = x.shape
    depth = w_in.shape[0]
    ret_w = ret_norm_g.shape[1]
    sb_w = sb_norm_g.shape[1]
    n_ret, n_sb = ret_w // HEAD_DIM, sb_w // HEAD_DIM
    c_rq, c_rk, c_rv, c_rg = 0, ret_w, 2 * ret_w, 3 * ret_w
    c_sq, c_sk, c_sv = 4 * ret_w, 4 * ret_w + sb_w, 4 * ret_w + 2 * sb_w

    cos_t, sin_t = _rotary_tables(s)
    dmat, qdec, kdec, sdec = _retention_tables(n_ret)
    tri = _after_matrix()

    xc = x.reshape(b * s, d)
    for l in range(depth):
        proj = _inproj(xc, norm1_g[l][None], w_in, l, cos_t, sin_t,
                       seq=s, n_rot_cols=2 * ret_w, scaled_cols=(c_sq, c_sk))
        proj = proj.reshape(b, s, -1)
        ret, (wo_b, wg_b, wu_b, wd_b) = _retention(
            proj, dmat, qdec, kdec, sdec, ret_norm_g[l].reshape(n_ret, 1, HEAD_DIM),
            (w_out, w_gate, w_up, w_down), l,
            n_heads=n_ret, col_q=c_rq, col_k=c_rk, col_v=c_rv, col_g=c_rg)
        sb = _stick_breaking(proj, tri, sb_norm_g[l][None],
                             n_heads=n_sb, col_q=c_sq, col_k=c_sk, col_v=c_sv)
        xc = _mix_ffn(xc, ret.reshape(b * s, ret_w), sb.reshape(b * s, sb_w),
                      wo_b, norm2_g[l][None], wg_b, wu_b, wd_b, final_g[None],
                      final_norm=(l == depth - 1))
    return xc.reshape(b, s, d)
```

```python
import functools

import jax
import jax.numpy as jnp
from jax import lax
from jax.experimental import pallas as pl
from jax.experimental.pallas import tpu as pltpu

HEAD_DIM = 128
ROPE_BASE = 10000.0
EPS = 1e-6
LOG2_E = 1.4426950408889634
RET_BLOCK = 128
RET_CHUNK = 64
RET_UNROLL = 4
SB_TILE = 256
FFN_TILE = 512
MIX_ROWS = 256
SB_DEAD_LOG2 = 151.0
SB_EXP_CLAMP = 126.0
VMEM_LIMIT = 48 * 1024 * 1024

F32 = jnp.float32
BF16 = jnp.bfloat16
_NT = (((1,), (1,)), ((), ()))
_TN = (((0,), (0,)), ((), ()))


def _sigmoid(x):
    return 1.0 / (1.0 + jnp.exp(-x))


def _inproj_kernel(x_ref, g_ref, w_ref, cos_ref, sin_ref, o_ref, wb_ref, *,
                   n_rot_cols, scaled_cols, scale, chunk):
    @pl.when(pl.program_id(0) == 0)
    def _():
        for c0 in range(0, wb_ref.shape[1], chunk):
            wb_ref[:, c0:c0 + chunk] = w_ref[:, c0:c0 + chunk].astype(BF16)

    xf = x_ref[...]
    y = xf * lax.rsqrt(jnp.mean(xf * xf, axis=-1, keepdims=True) + EPS)
    h = (y * g_ref[...]).astype(BF16)
    cos = cos_ref[...]
    sin = sin_ref[...]
    for c0 in range(0, wb_ref.shape[1], chunk):
        acc = jnp.dot(h, wb_ref[:, c0:c0 + chunk], preferred_element_type=F32)
        for h0 in range(0, chunk, HEAD_DIM):
            a = acc[:, h0:h0 + HEAD_DIM]
            col = c0 + h0
            if col < n_rot_cols:
                a = a * cos + pltpu.roll(a, HEAD_DIM // 2, 1) * sin
            elif scaled_cols[0] <= col < scaled_cols[1]:
                a = a * scale
            o_ref[:, col:col + HEAD_DIM] = a.astype(BF16)


def _inproj(x2d, g, w_all, layer, cos_t, sin_t, *, seq, n_rot_cols, scaled_cols,
            tm=512, chunk=512):
    t, d = x2d.shape
    n = w_all.shape[2]
    assert t % tm == 0 and seq % tm == 0 and n % chunk == 0 and chunk % HEAD_DIM == 0
    pos_blocks = seq // tm
    return pl.pallas_call(
        functools.partial(_inproj_kernel, n_rot_cols=n_rot_cols, scaled_cols=scaled_cols,
                          scale=HEAD_DIM ** -0.5 * LOG2_E, chunk=chunk),
        grid=(t // tm,),
        in_specs=[
            pl.BlockSpec((tm, d), lambda i: (i, 0)),
            pl.BlockSpec((1, d), lambda i: (0, 0)),
            pl.BlockSpec((None, d, n), lambda i: (layer, 0, 0), pipeline_mode=pl.Buffered(1)),
            pl.BlockSpec((tm, HEAD_DIM), lambda i: (i % pos_blocks, 0)),
            pl.BlockSpec((tm, HEAD_DIM), lambda i: (i % pos_blocks, 0)),
        ],
        out_specs=pl.BlockSpec((tm, n), lambda i: (i, 0)),
        out_shape=jax.ShapeDtypeStruct((t, n), BF16),
        scratch_shapes=[pltpu.VMEM((d, n), BF16)],
        compiler_params=pltpu.CompilerParams(
            dimension_semantics=("arbitrary",), vmem_limit_bytes=VMEM_LIMIT),
        name="inproj",
    )(x2d, g, w_all, cos_t, sin_t)


def _ret_kernel(q_ref, k_ref, v_ref, gate_ref, dmat_ref, qdec_ref, kdec_ref, sdec_ref, gn_ref,
                *rest, n_heads, n_trips, n_cast):
    hd = HEAD_DIM
    o_ref = rest[n_cast]
    for src_ref, dst_ref in zip(rest[:n_cast], rest[n_cast + 1:]):
        dst_ref[...] = src_ref[...].astype(BF16)

    def step(r, hh, state):
        c = slice(hh * hd, (hh + 1) * hd)
        q = q_ref[pl.ds(r, RET_BLOCK), c]
        k = k_ref[pl.ds(r, RET_BLOCK), c]
        v = v_ref[pl.ds(r, RET_BLOCK), c]
        s = lax.dot_general(q, k, _NT, preferred_element_type=F32) * dmat_ref[hh]
        intra = jnp.dot(s.astype(BF16), v, preferred_element_type=F32)
        qd = (q.astype(F32) * qdec_ref[hh]).astype(BF16)
        cross = jnp.dot(qd, state.astype(BF16), preferred_element_type=F32)
        kd = (k.astype(F32) * kdec_ref[hh]).astype(BF16)
        kv = lax.dot_general(kd, v, _TN, preferred_element_type=F32)
        o = intra + cross
        mu = jnp.mean(o, axis=-1, keepdims=True)
        oc = o - mu
        var = jnp.mean(oc * oc, axis=-1, keepdims=True)
        y = oc * lax.rsqrt(var + EPS) * gn_ref[hh]
        gate = gate_ref[pl.ds(r, RET_BLOCK), c].astype(F32)
        o_ref[pl.ds(r, RET_BLOCK), c] = (y * (gate * _sigmoid(gate))).astype(BF16)
        return state * sdec_ref[hh] + kv

    def body(n, states):
        states = list(states)
        for u in range(RET_UNROLL):
            r = pl.multiple_of((n * RET_UNROLL + u) * RET_BLOCK, RET_BLOCK)
            for hh in range(n_heads):
                states[hh] = step(r, hh, states[hh])
        return tuple(states)

    lax.fori_loop(0, n_trips, body,
                  tuple(jnp.zeros((hd, hd), F32) for _ in range(n_heads)))


def _retention(proj, dmat, qdec, kdec, sdec, gn, cast_weights, layer, *,
               n_heads, col_q, col_k, col_v, col_g):
    b, s, _ = proj.shape
    w = n_heads * HEAD_DIM
    assert s % (RET_BLOCK * RET_UNROLL) == 0
    assert col_q % w == 0 and col_k % w == 0 and col_v % w == 0 and col_g % w == 0
    assert all(a.shape[1] % (b * 16) == 0 for a in cast_weights)

    def col(c0):
        return pl.BlockSpec((None, s, w), lambda bi: (bi, 0, c0 // w))

    def table(a):
        return pl.BlockSpec(a.shape, lambda bi: (0, 0, 0))

    def slab_in(a):
        return pl.BlockSpec((None, a.shape[1] // b, a.shape[2]), lambda bi: (layer, bi, 0))

    def slab_out(a):
        return pl.BlockSpec((a.shape[1] // b, a.shape[2]), lambda bi: (bi, 0))

    outs = pl.pallas_call(
        functools.partial(_ret_kernel, n_heads=n_heads, n_trips=s // (RET_BLOCK * RET_UNROLL),
                          n_cast=len(cast_weights)),
        grid=(b,),
        in_specs=[col(col_q), col(col_k), col(col_v), col(col_g),
                  table(dmat), table(qdec), table(kdec), table(sdec), table(gn)]
                 + [slab_in(a) for a in cast_weights],
        out_specs=[pl.BlockSpec((None, s, w), lambda bi: (bi, 0, 0))]
                  + [slab_out(a) for a in cast_weights],
        out_shape=[jax.ShapeDtypeStruct((b, s, w), BF16)]
                  + [jax.ShapeDtypeStruct(a.shape[1:], BF16) for a in cast_weights],
        compiler_params=pltpu.CompilerParams(
            dimension_semantics=("arbitrary",), vmem_limit_bytes=VMEM_LIMIT),
        name="retention",
    )(proj, proj, proj, proj, dmat, qdec, kdec, sdec, gn, *cast_weights)
    return outs[0], outs[1:]


def _sb_kernel(q_ref, k_ref, v_ref, tri_ref, gn_ref, o_ref, acc_ref, carry_ref, *, n_heads):
    qi = pl.program_id(1)
    t = SB_TILE
    hd = HEAD_DIM
    row = lax.broadcasted_iota(jnp.int32, (t, t), 0)
    colm = lax.broadcasted_iota(jnp.int32, (t, t), 1)
    valid = colm < row

    def tile(j, masked):
        r = pl.multiple_of(j * t, t)
        heads = [slice(hh * hd, (hh + 1) * hd) for hh in range(n_heads)]
        hilo, logb, rows = [], [], []
        for c in heads:
            z = lax.dot_general(q_ref[:, c], k_ref[pl.ds(r, t), c], _NT,
                                preferred_element_type=F32)
            sp = jnp.maximum(z, jnp.log2(1.0 + jnp.exp2(jnp.minimum(z, SB_EXP_CLAMP))))
            spm = jnp.where(valid, sp, 0.0) if masked else sp
            hi = spm.astype(BF16)
            lo = (spm - hi.astype(F32)).astype(BF16)
            hilo.append(jnp.concatenate([hi, lo], axis=1))
            logb.append(z - sp)
            rows.append(jnp.sum(spm, axis=-1, keepdims=True))
        ws = []
        for hh in range(n_heads):
            after = jnp.dot(hilo[hh], tri_ref[...], preferred_element_type=F32)
            w = jnp.exp2(logb[hh] - after)
            if masked:
                w = jnp.where(valid, w, 0.0)
            ws.append(w.astype(BF16))
        for hh, c in enumerate(heads):
            pv = jnp.dot(ws[hh], v_ref[pl.ds(r, t), c], preferred_element_type=F32)
            if masked:
                acc_ref[:, c] = pv
                carry_ref[:, c] = jnp.broadcast_to(rows[hh], (t, hd))
            else:
                carry = carry_ref[:, c]
                acc_ref[:, c] += jnp.exp2(-carry) * pv
                carry_ref[:, c] = carry + rows[hh]

    tile(qi, True)

    def more(state):
        it, least = state
        return (it < qi) & (least < SB_DEAD_LOG2)

    def body(state):
        it, _ = state
        tile(qi - 1 - it, False)
        return it + 1, jnp.min(carry_ref[...])

    lax.while_loop(more, body, (jnp.int32(0), jnp.min(carry_ref[...])))

    for hh in range(n_heads):
        c = slice(hh * hd, (hh + 1) * hd)
        acc = acc_ref[:, c]
        y = acc * lax.rsqrt(jnp.mean(acc * acc, axis=-1, keepdims=True) + EPS)
        o_ref[:, c] = (y * gn_ref[:, c]).astype(BF16)


def _stick_breaking(proj, tri, gn, *, n_heads, col_q, col_k, col_v):
    b, s, _ = proj.shape
    w = n_heads * HEAD_DIM
    t = SB_TILE
    assert s % t == 0 and col_q % w == 0 and col_k % w == 0 and col_v % w == 0
    return pl.pallas_call(
        functools.partial(_sb_kernel, n_heads=n_heads),
        grid=(b, s // t),
        in_specs=[
            pl.BlockSpec((None, t, w), lambda bi, qi: (bi, qi, col_q // w)),
            pl.BlockSpec((None, s, w), lambda bi, qi: (bi, 0, col_k // w)),
            pl.BlockSpec((None, s, w), lambda bi, qi: (bi, 0, col_v // w)),
            pl.BlockSpec(tri.shape, lambda bi, qi: (0, 0)),
            pl.BlockSpec((1, w), lambda bi, qi: (0, 0)),
        ],
        out_specs=pl.BlockSpec((None, t, w), lambda bi, qi: (bi, qi, 0)),
        out_shape=jax.ShapeDtypeStruct((b, s, w), BF16),
        scratch_shapes=[pltpu.VMEM((t, w), F32), pltpu.VMEM((t, w), F32)],
        compiler_params=pltpu.CompilerParams(
            dimension_semantics=("arbitrary", "arbitrary"), vmem_limit_bytes=VMEM_LIMIT),
        name="stick_breaking",
    )(proj, proj, proj, tri, gn)


def _mix_ffn_kernel(x_ref, ret_ref, sb_ref, wo_ref, g2_ref, wg_ref, wu_ref, wd_ref, fg_ref,
                    o_ref, acc_ref, h_ref, *, final_norm, tf):
    ka = ret_ref.shape[1]
    tm = x_ref.shape[0]
    ff = wd_ref.shape[0]
    for r0 in range(0, tm, MIX_ROWS):
        rows = slice(r0, r0 + MIX_ROWS)
        x1 = (x_ref[rows, :]
              + jnp.dot(ret_ref[rows, :], wo_ref[:ka, :], preferred_element_type=F32)
              + jnp.dot(sb_ref[rows, :], wo_ref[ka:, :], preferred_element_type=F32))
        acc_ref[rows, :] = x1
        y = x1 * lax.rsqrt(jnp.mean(x1 * x1, axis=-1, keepdims=True) + EPS)
        h_ref[rows, :] = (y * g2_ref[...]).astype(BF16)

    def ff_tile(f0, width):
        gt = jnp.dot(h_ref[...], wg_ref[:, pl.ds(f0, width)], preferred_element_type=F32)
        up = jnp.dot(h_ref[...], wu_ref[:, pl.ds(f0, width)], preferred_element_type=F32)
        a = (gt * _sigmoid(gt) * up).astype(BF16)
        acc_ref[...] += jnp.dot(a, wd_ref[pl.ds(f0, width), :], preferred_element_type=F32)

    def ff_body(f, _):
        ff_tile(pl.multiple_of(f * tf, tf), tf)
        return 0

    n_full = ff // tf
    lax.fori_loop(0, n_full, ff_body, 0)
    if ff % tf:
        ff_tile(n_full * tf, ff % tf)
    x2 = acc_ref[...]
    if final_norm:
        y = x2 * lax.rsqrt(jnp.mean(x2 * x2, axis=-1, keepdims=True) + EPS)
        o_ref[...] = y * fg_ref[...]
    else:
        o_ref[...] = x2


def _mix_ffn(x2d, ret2d, sb2d, wo, g2, wg, wu, wd, fg, *, final_norm, tm=512, tf=FFN_TILE):
    t, d = x2d.shape
    ka, kb = ret2d.shape[1], sb2d.shape[1]
    ff = wg.shape[1]
    assert t % tm == 0 and tm % MIX_ROWS == 0 and wo.shape[0] == ka + kb
    assert tf % 256 == 0 and (ff % tf) % 256 == 0

    def resident(a):
        return pl.BlockSpec(a.shape, lambda i: (0, 0), pipeline_mode=pl.Buffered(1))

    return pl.pallas_call(
        functools.partial(_mix_ffn_kernel, final_norm=final_norm, tf=tf),
        grid=(t // tm,),
        in_specs=[
            pl.BlockSpec((tm, d), lambda i: (i, 0)),
            pl.BlockSpec((tm, ka), lambda i: (i, 0)),
            pl.BlockSpec((tm, kb), lambda i: (i, 0)),
            resident(wo),
            pl.BlockSpec((1, d), lambda i: (0, 0)),
            resident(wg), resident(wu), resident(wd),
            pl.BlockSpec((1, d), lambda i: (0, 0)),
        ],
        out_specs=pl.BlockSpec((tm, d), lambda i: (i, 0)),
        out_shape=jax.ShapeDtypeStruct((t, d), F32),
        scratch_shapes=[pltpu.VMEM((tm, d), F32), pltpu.VMEM((tm, d), BF16)],
        compiler_params=pltpu.CompilerParams(
            dimension_semantics=("arbitrary",), vmem_limit_bytes=VMEM_LIMIT),
        name="mix_ffn",
    )(x2d, ret2d, sb2d, wo, g2, wg, wu, wd, fg)


def _rotary_tables(seq):
    d = HEAD_DIM
    inv_freq = 1.0 / (ROPE_BASE ** (jnp.arange(0, d, 2, dtype=F32) / d))
    ang = jnp.arange(seq, dtype=F32)[:, None] * inv_freq[None, :]
    cos = jnp.cos(ang)
    sin = jnp.sin(ang)
    return jnp.concatenate([cos, cos], axis=-1), jnp.concatenate([-sin, sin], axis=-1)


def _retention_tables(n_heads):
    c = RET_BLOCK
    scale = HEAD_DIM ** -0.5
    log_g = jnp.log1p(-jnp.exp2(-5.0 - jnp.arange(n_heads, dtype=F32)))
    i = jnp.arange(c, dtype=F32)
    diff = i[:, None] - i[None, :]
    ci = jnp.arange(c) // RET_CHUNK
    same = ci[:, None] == ci[None, :]
    earlier = ci[None, :] < ci[:, None]
    dist = jnp.where(same, jnp.abs(diff), diff)
    dmat = jnp.where((same | earlier)[None], jnp.exp(log_g[:, None, None] * dist[None]), 0.0) * scale
    ones = jnp.ones((1, 1, HEAD_DIM), F32)
    qdec = jnp.exp(log_g[:, None] * (i + 1.0))[..., None] * ones
    kdec = jnp.exp(log_g[:, None] * (c - 1.0 - i))[..., None] * scale * ones
    sdec = jnp.exp(log_g * c)[:, None, None] * ones
    return dmat.astype(F32), qdec, kdec, sdec


def _after_matrix():
    j = jnp.arange(SB_TILE)
    tri = (j[:, None] > j[None, :]).astype(BF16)
    return jnp.concatenate([tri, tri], axis=0)


def kernel(x, norm1_g, w_in, ret_norm_g, sb_norm_g, w_out, norm2_g, w_gate, w_up, w_down, final_g):
    b, s, d = x.shape
    depth = w_in.shape[0]
    ret_w = ret_norm_g.shape[1]
    sb_w = sb_norm_g.shape[1]
    n_ret, n_sb = ret_w // HEAD_DIM, sb_w // HEAD_DIM
    c_rq, c_rk, c_rv, c_rg = 0, ret_w, 2 * ret_w, 3 * ret_w
    c_sq, c_sk, c_sv = 4 * ret_w, 4 * ret_w + sb_w, 4 * ret_w + 2 * sb_w

    cos_t, sin_t = _rotary_tables(s)
    dmat, qdec, kdec, sdec = _retention_tables(n_ret)
    tri = _after_matrix()

    xc = x.reshape(b * s, d)
    for l in range(depth):
        proj = _inproj(xc, norm1_g[l][None], w_in, l, cos_t, sin_t,
                       seq=s, n_rot_cols=2 * ret_w, scaled_cols=(c_sq, c_sk))
        proj = proj.reshape(b, s, -1)
        ret, (wo_b, wg_b, wu_b, wd_b) = _retention(
            proj, dmat, qdec, kdec, sdec, ret_norm_g[l].reshape(n_ret, 1, HEAD_DIM),
            (w_out, w_gate, w_up, w_down), l,
            n_heads=n_ret, col_q=c_rq, col_k=c_rk, col_v=c_rv, col_g=c_rg)
        sb = _stick_breaking(proj, tri, sb_norm_g[l][None],
                             n_heads=n_sb, col_q=c_sq, col_k=c_sk, col_v=c_sv)
        xc = _mix_ffn(xc, ret.reshape(b * s, ret_w), sb.reshape(b * s, sb_w),
                      wo_b, norm2_g[l][None], wg_b, wu_b, wd_b, final_g[None],
                      final_norm=(l == depth - 1))
    return xc.reshape(b, s, d)
```

```python
import functools

import jax
import jax.numpy as jnp
from jax import lax
from jax.experimental import pallas as pl
from jax.experimental.pallas import tpu as pltpu

HEAD_DIM = 128
ROPE_BASE = 10000.0
EPS = 1e-6
LOG2_E = 1.4426950408889634
RET_BLOCK = 128
RET_CHUNK = 64
RET_UNROLL = 4
SB_TILE = 256
FFN_TILE = 256
MIX_ROWS = 256
SB_DEAD_LOG2 = 151.0
SB_EXP_CLAMP = 126.0
VMEM_LIMIT = 56 * 1024 * 1024

F32 = jnp.float32
BF16 = jnp.bfloat16
_NT = (((1,), (1,)), ((), ()))
_TN = (((0,), (0,)), ((), ()))


def _sigmoid(x):
    return 1.0 / (1.0 + jnp.exp(-x))


def _inproj_kernel(x_ref, g_ref, w_ref, cos_ref, sin_ref, o_ref, wb_ref, *,
                   n_rot_cols, scaled_cols, scale, chunk):
    @pl.when(pl.program_id(0) == 0)
    def _():
        for c0 in range(0, wb_ref.shape[1], chunk):
            wb_ref[:, c0:c0 + chunk] = w_ref[:, c0:c0 + chunk].astype(BF16)

    xf = x_ref[...]
    y = xf * lax.rsqrt(jnp.mean(xf * xf, axis=-1, keepdims=True) + EPS)
    h = (y * g_ref[...]).astype(BF16)
    cos = cos_ref[...]
    sin = sin_ref[...]
    for c0 in range(0, wb_ref.shape[1], chunk):
        acc = jnp.dot(h, wb_ref[:, c0:c0 + chunk], preferred_element_type=F32)
        for h0 in range(0, chunk, HEAD_DIM):
            a = acc[:, h0:h0 + HEAD_DIM]
            col = c0 + h0
            if col < n_rot_cols:
                a = a * cos + pltpu.roll(a, HEAD_DIM // 2, 1) * sin
            elif scaled_cols[0] <= col < scaled_cols[1]:
                a = a * scale
            o_ref[:, col:col + HEAD_DIM] = a.astype(BF16)


def _inproj(x2d, g, w_all, layer, cos_t, sin_t, *, seq, n_rot_cols, scaled_cols,
            tm=512, chunk=512):
    t, d = x2d.shape
    n = w_all.shape[2]
    assert t % tm == 0 and seq % tm == 0 and n % chunk == 0 and chunk % HEAD_DIM == 0
    pos_blocks = seq // tm
    return pl.pallas_call(
        functools.partial(_inproj_kernel, n_rot_cols=n_rot_cols, scaled_cols=scaled_cols,
                          scale=HEAD_DIM ** -0.5 * LOG2_E, chunk=chunk),
        grid=(t // tm,),
        in_specs=[
            pl.BlockSpec((tm, d), lambda i: (i, 0)),
            pl.BlockSpec((1, d), lambda i: (0, 0)),
            pl.BlockSpec((None, d, n), lambda i: (layer, 0, 0), pipeline_mode=pl.Buffered(1)),
            pl.BlockSpec((tm, HEAD_DIM), lambda i: (i % pos_blocks, 0)),
            pl.BlockSpec((tm, HEAD_DIM), lambda i: (i % pos_blocks, 0)),
        ],
        out_specs=pl.BlockSpec((tm, n), lambda i: (i, 0)),
        out_shape=jax.ShapeDtypeStruct((t, n), BF16),
        scratch_shapes=[pltpu.VMEM((d, n), BF16)],
        compiler_params=pltpu.CompilerParams(
            dimension_semantics=("arbitrary",), vmem_limit_bytes=VMEM_LIMIT),
        name="inproj",
    )(x2d, g, w_all, cos_t, sin_t)


def _ret_kernel(q_ref, k_ref, v_ref, gate_ref, dmat_ref, qdec_ref, kdec_ref, sdec_ref, gn_ref,
                *rest, n_heads, n_trips, n_cast):
    hd = HEAD_DIM
    o_ref = rest[n_cast]
    for src_ref, dst_ref in zip(rest[:n_cast], rest[n_cast + 1:]):
        dst_ref[...] = src_ref[...].astype(BF16)

    def step(r, hh, state):
        c = slice(hh * hd, (hh + 1) * hd)
        q = q_ref[pl.ds(r, RET_BLOCK), c]
        k = k_ref[pl.ds(r, RET_BLOCK), c]
        v = v_ref[pl.ds(r, RET_BLOCK), c]
        s = lax.dot_general(q, k, _NT, preferred_element_type=F32) * dmat_ref[hh]
        intra = jnp.dot(s.astype(BF16), v, preferred_element_type=F32)
        qd = (q.astype(F32) * qdec_ref[hh]).astype(BF16)
        cross = jnp.dot(qd, state.astype(BF16), preferred_element_type=F32)
        kd = (k.astype(F32) * kdec_ref[hh]).astype(BF16)
        kv = lax.dot_general(kd, v, _TN, preferred_element_type=F32)
        o = intra + cross
        mu = jnp.mean(o, axis=-1, keepdims=True)
        oc = o - mu
        var = jnp.mean(oc * oc, axis=-1, keepdims=True)
        y = oc * lax.rsqrt(var + EPS) * gn_ref[hh]
        gate = gate_ref[pl.ds(r, RET_BLOCK), c].astype(F32)
        o_ref[pl.ds(r, RET_BLOCK), c] = (y * (gate * _sigmoid(gate))).astype(BF16)
        return state * sdec_ref[hh] + kv

    def body(n, states):
        states = list(states)
        for u in range(RET_UNROLL):
            r = pl.multiple_of((n * RET_UNROLL + u) * RET_BLOCK, RET_BLOCK)
            for hh in range(n_heads):
                states[hh] = step(r, hh, states[hh])
        return tuple(states)

    lax.fori_loop(0, n_trips, body,
                  tuple(jnp.zeros((hd, hd), F32) for _ in range(n_heads)))


def _retention(proj, dmat, qdec, kdec, sdec, gn, cast_weights, layer, *,
               n_heads, col_q, col_k, col_v, col_g):
    b, s, _ = proj.shape
    w = n_heads * HEAD_DIM
    assert s % (RET_BLOCK * RET_UNROLL) == 0
    assert col_q % w == 0 and col_k % w == 0 and col_v % w == 0 and col_g % w == 0
    assert all(a.shape[1] % (b * 16) == 0 for a in cast_weights)

    def col(c0):
        return pl.BlockSpec((None, s, w), lambda bi: (bi, 0, c0 // w))

    def table(a):
        return pl.BlockSpec(a.shape, lambda bi: (0, 0, 0))

    def slab_in(a):
        return pl.BlockSpec((None, a.shape[1] // b, a.shape[2]), lambda bi: (layer, bi, 0))

    def slab_out(a):
        return pl.BlockSpec((a.shape[1] // b, a.shape[2]), lambda bi: (bi, 0))

    outs = pl.pallas_call(
        functools.partial(_ret_kernel, n_heads=n_heads, n_trips=s // (RET_BLOCK * RET_UNROLL),
                          n_cast=len(cast_weights)),
        grid=(b,),
        in_specs=[col(col_q), col(col_k), col(col_v), col(col_g),
                  table(dmat), table(qdec), table(kdec), table(sdec), table(gn)]
                 + [slab_in(a) for a in cast_weights],
        out_specs=[pl.BlockSpec((None, s, w), lambda bi: (bi, 0, 0))]
                  + [slab_out(a) for a in cast_weights],
        out_shape=[jax.ShapeDtypeStruct((b, s, w), BF16)]
                  + [jax.ShapeDtypeStruct(a.shape[1:], BF16) for a in cast_weights],
        compiler_params=pltpu.CompilerParams(
            dimension_semantics=("arbitrary",), vmem_limit_bytes=VMEM_LIMIT),
        name="retention",
    )(proj, proj, proj, proj, dmat, qdec, kdec, sdec, gn, *cast_weights)
    return outs[0], outs[1:]


def _sb_kernel(q_ref, k_ref, v_ref, tri_ref, gn_ref, o_ref, acc_ref, carry_ref, *, n_heads):
    qi = pl.program_id(1)
    t = SB_TILE
    hd = HEAD_DIM
    row = lax.broadcasted_iota(jnp.int32, (t, t), 0)
    colm = lax.broadcasted_iota(jnp.int32, (t, t), 1)
    valid = colm < row

    def tile(j, masked):
        r = pl.multiple_of(j * t, t)
        heads = [slice(hh * hd, (hh + 1) * hd) for hh in range(n_heads)]
        keep, logb, rows, ws = {}, {}, {}, {}

        def scores(hh):
            c = heads[hh]
            z = lax.dot_general(q_ref[:, c], k_ref[pl.ds(r, t), c], _NT,
                                preferred_element_type=F32)
            sp = jnp.maximum(z, jnp.log2(1.0 + jnp.exp2(jnp.minimum(z, SB_EXP_CLAMP))))
            spm = jnp.where(valid, sp, 0.0) if masked else sp
            keep[hh] = spm.astype(BF16)
            logb[hh] = z - sp
            rows[hh] = jnp.sum(spm, axis=-1, keepdims=True)

        def weights(hh):
            after = jnp.dot(keep[hh], tri_ref[...], preferred_element_type=F32)
            w = jnp.exp2(logb[hh] - after)
            if masked:
                w = jnp.where(valid, w, 0.0)
            ws[hh] = w.astype(BF16)

        def accumulate(hh):
            c = heads[hh]
            pv = jnp.dot(ws[hh], v_ref[pl.ds(r, t), c], preferred_element_type=F32)
            if masked:
                acc_ref[:, c] = pv
                carry_ref[:, c] = jnp.broadcast_to(rows[hh], (t, hd))
            else:
                carry = carry_ref[:, c]
                acc_ref[:, c] += jnp.exp2(-carry) * pv
                carry_ref[:, c] = carry + rows[hh]

        for stage in (scores, weights, accumulate):
            for hh in range(n_heads):
                stage(hh)

    tile(qi, True)

    def more(state):
        it, least = state
        return (it < qi) & (least < SB_DEAD_LOG2)

    def body(state):
        it, _ = state
        tile(qi - 1 - it, False)
        return it + 1, jnp.min(carry_ref[...])

    lax.while_loop(more, body, (jnp.int32(0), jnp.min(carry_ref[...])))

    for hh in range(n_heads):
        c = slice(hh * hd, (hh + 1) * hd)
        acc = acc_ref[:, c]
        y = acc * lax.rsqrt(jnp.mean(acc * acc, axis=-1, keepdims=True) + EPS)
        o_ref[:, c] = (y * gn_ref[:, c]).astype(BF16)


def _stick_breaking(proj, tri, gn, *, n_heads, col_q, col_k, col_v):
    b, s, _ = proj.shape
    w = n_heads * HEAD_DIM
    t = SB_TILE
    assert s % t == 0 and col_q % w == 0 and col_k % w == 0 and col_v % w == 0
    return pl.pallas_call(
        functools.partial(_sb_kernel, n_heads=n_heads),
        grid=(b, s // t),
        in_specs=[
            pl.BlockSpec((None, t, w), lambda bi, qi: (bi, qi, col_q // w)),
            pl.BlockSpec((None, s, w), lambda bi, qi: (bi, 0, col_k // w)),
            pl.BlockSpec((None, s, w), lambda bi, qi: (bi, 0, col_v // w)),
            pl.BlockSpec(tri.shape, lambda bi, qi: (0, 0)),
            pl.BlockSpec((1, w), lambda bi, qi: (0, 0)),
        ],
        out_specs=pl.BlockSpec((None, t, w), lambda bi, qi: (bi, qi, 0)),
        out_shape=jax.ShapeDtypeStruct((b, s, w), BF16),
        scratch_shapes=[pltpu.VMEM((t, w), F32), pltpu.VMEM((t, w), F32)],
        compiler_params=pltpu.CompilerParams(
            dimension_semantics=("arbitrary", "arbitrary"), vmem_limit_bytes=VMEM_LIMIT),
        name="stick_breaking",
    )(proj, proj, proj, tri, gn)


def _mix_ffn_kernel(x_ref, ret_ref, sb_ref, wo_ref, g2_ref, wg_ref, wu_ref, wd_ref, fg_ref,
                    o_ref, acc_ref, h_ref, *, final_norm, tf):
    ka = ret_ref.shape[1]
    tm = x_ref.shape[0]
    ff = wd_ref.shape[0]
    for r0 in range(0, tm, MIX_ROWS):
        rows = slice(r0, r0 + MIX_ROWS)
        x1 = (x_ref[rows, :]
              + jnp.dot(ret_ref[rows, :], wo_ref[:ka, :], preferred_element_type=F32)
              + jnp.dot(sb_ref[rows, :], wo_ref[ka:, :], preferred_element_type=F32))
        acc_ref[rows, :] = x1
        y = x1 * lax.rsqrt(jnp.mean(x1 * x1, axis=-1, keepdims=True) + EPS)
        h_ref[rows, :] = (y * g2_ref[...]).astype(BF16)

    def ff_tile(f0, width):
        gt = jnp.dot(h_ref[...], wg_ref[:, pl.ds(f0, width)], preferred_element_type=F32)
        up = jnp.dot(h_ref[...], wu_ref[:, pl.ds(f0, width)], preferred_element_type=F32)
        a = (gt * _sigmoid(gt) * up).astype(BF16)
        acc_ref[...] += jnp.dot(a, wd_ref[pl.ds(f0, width), :], preferred_element_type=F32)

    def ff_body(f, _):
        ff_tile(pl.multiple_of(f * tf, tf), tf)
        return 0

    n_full = ff // tf
    lax.fori_loop(0, n_full, ff_body, 0)
    if ff % tf:
        ff_tile(n_full * tf, ff % tf)
    x2 = acc_ref[...]
    if final_norm:
        y = x2 * lax.rsqrt(jnp.mean(x2 * x2, axis=-1, keepdims=True) + EPS)
        o_ref[...] = y * fg_ref[...]
    else:
        o_ref[...] = x2


def _mix_ffn(x2d, ret2d, sb2d, wo, g2, wg, wu, wd, fg, *, final_norm, tm=1024, tf=FFN_TILE):
    t, d = x2d.shape
    ka, kb = ret2d.shape[1], sb2d.shape[1]
    ff = wg.shape[1]
    assert t % tm == 0 and tm % MIX_ROWS == 0 and wo.shape[0] == ka + kb
    assert tf % 256 == 0 and (ff % tf) % 256 == 0

    def resident(a):
        return pl.BlockSpec(a.shape, lambda i: (0, 0), pipeline_mode=pl.Buffered(1))

    return pl.pallas_call(
        functools.partial(_mix_ffn_kernel, final_norm=final_norm, tf=tf),
        grid=(t // tm,),
        in_specs=[
            pl.BlockSpec((tm, d), lambda i: (i, 0)),
            pl.BlockSpec((tm, ka), lambda i: (i, 0)),
            pl.BlockSpec((tm, kb), lambda i: (i, 0)),
            resident(wo),
            pl.BlockSpec((1, d), lambda i: (0, 0)),
            resident(wg), resident(wu), resident(wd),
            pl.BlockSpec((1, d), lambda i: (0, 0)),
        ],
        out_specs=pl.BlockSpec((tm, d), lambda i: (i, 0)),
        out_shape=jax.ShapeDtypeStruct((t, d), F32),
        scratch_shapes=[pltpu.VMEM((tm, d), F32), pltpu.VMEM((tm, d), BF16)],
        compiler_params=pltpu.CompilerParams(
            dimension_semantics=("arbitrary",), vmem_limit_bytes=VMEM_LIMIT),
        name="mix_ffn",
    )(x2d, ret2d, sb2d, wo, g2, wg, wu, wd, fg)


def _rotary_tables(seq):
    d = HEAD_DIM
    inv_freq = 1.0 / (ROPE_BASE ** (jnp.arange(0, d, 2, dtype=F32) / d))
    ang = jnp.arange(seq, dtype=F32)[:, None] * inv_freq[None, :]
    cos = jnp.cos(ang)
    sin = jnp.sin(ang)
    return jnp.concatenate([cos, cos], axis=-1), jnp.concatenate([-sin, sin], axis=-1)


def _retention_tables(n_heads):
    c = RET_BLOCK
    scale = HEAD_DIM ** -0.5
    log_g = jnp.log1p(-jnp.exp2(-5.0 - jnp.arange(n_heads, dtype=F32)))
    i = jnp.arange(c, dtype=F32)
    diff = i[:, None] - i[None, :]
    ci = jnp.arange(c) // RET_CHUNK
    same = ci[:, None] == ci[None, :]
    earlier = ci[None, :] < ci[:, None]
    dist = jnp.where(same, jnp.abs(diff), diff)
    dmat = jnp.where((same | earlier)[None], jnp.exp(log_g[:, None, None] * dist[None]), 0.0) * scale
    ones = jnp.ones((1, 1, HEAD_DIM), F32)
    qdec = jnp.exp(log_g[:, None] * (i + 1.0))[..., None] * ones
    kdec = jnp.exp(log_g[:, None] * (c - 1.0 - i))[..., None] * scale * ones
    sdec = jnp.exp(log_g * c)[:, None, None] * ones
    return dmat.astype(F32), qdec, kdec, sdec


def _after_matrix():
    j = jnp.arange(SB_TILE)
    return (j[:, None] > j[None, :]).astype(BF16)


def kernel(x, norm1_g, w_in, ret_norm_g, sb_norm_g, w_out, norm2_g, w_gate, w_up, w_down, final_g):
    b, s, d = x.shape
    depth = w_in.shape[0]
    ret_w = ret_norm_g.shape[1]
    sb_w = sb_norm_g.shape[1]
    n_ret, n_sb = ret_w // HEAD_DIM, sb_w // HEAD_DIM
    c_rq, c_rk, c_rv, c_rg = 0, ret_w, 2 * ret_w, 3 * ret_w
    c_sq, c_sk, c_sv = 4 * ret_w, 4 * ret_w + sb_w, 4 * ret_w + 2 * sb_w

    cos_t, sin_t = _rotary_tables(s)
    dmat, qdec, kdec, sdec = _retention_tables(n_ret)
    tri = _after_matrix()

    xc = x.reshape(b * s, d)
    for l in range(depth):
        proj = _inproj(xc, norm1_g[l][None], w_in, l, cos_t, sin_t,
                       seq=s, n_rot_cols=2 * ret_w, scaled_cols=(c_sq, c_sk))
        proj = proj.reshape(b, s, -1)
        ret, (wo_b, wg_b, wu_b, wd_b) = _retention(
            proj, dmat, qdec, kdec, sdec, ret_norm_g[l].reshape(n_ret, 1, HEAD_DIM),
            (w_out, w_gate, w_up, w_down), l,
            n_heads=n_ret, col_q=c_rq, col_k=c_rk, col_v=c_rv, col_g=c_rg)
        sb = _stick_breaking(proj, tri, sb_norm_g[l][None],
                             n_heads=n_sb, col_q=c_sq, col_k=c_sk, col_v=c_sv)
        xc = _mix_ffn(xc, ret.reshape(b * s, ret_w), sb.reshape(b * s, sb_w),
                      wo_b, norm2_g[l][None], wg_b, wu_b, wd_b, final_g[None],
                      final_norm=(l == depth - 1))
    return xc.reshape(b, s, d)
```

```python
import functools

import jax
import jax.numpy as jnp
from jax import lax
from jax.experimental import pallas as pl
from jax.experimental.pallas import tpu as pltpu

HEAD_DIM = 128
ROPE_BASE = 10000.0
EPS = 1e-6
LOG2_E = 1.4426950408889634
RET_BLOCK = 128
RET_CHUNK = 64
RET_UNROLL = 4
SB_TILE = 256
FFN_TILE = 512
MIX_ROWS = 256
SB_DEAD_LOG2 = 151.0
SB_EXP_CLAMP = 126.0
VMEM_LIMIT = 56 * 1024 * 1024

F32 = jnp.float32
BF16 = jnp.bfloat16
_NT = (((1,), (1,)), ((), ()))
_TN = (((0,), (0,)), ((), ()))


def _sigmoid(x):
    return 1.0 / (1.0 + jnp.exp(-x))


def _inproj_kernel(x_ref, g_ref, w_ref, cos_ref, sin_ref, o_ref, wb_ref, *,
                   n_rot_cols, scaled_cols, scale, chunk):
    @pl.when(pl.program_id(0) == 0)
    def _():
        for c0 in range(0, wb_ref.shape[1], chunk):
            wb_ref[:, c0:c0 + chunk] = w_ref[:, c0:c0 + chunk].astype(BF16)

    xf = x_ref[...]
    y = xf * lax.rsqrt(jnp.mean(xf * xf, axis=-1, keepdims=True) + EPS)
    h = (y * g_ref[...]).astype(BF16)
    cos = cos_ref[...]
    sin = sin_ref[...]
    for c0 in range(0, wb_ref.shape[1], chunk):
        acc = jnp.dot(h, wb_ref[:, c0:c0 + chunk], preferred_element_type=F32)
        for h0 in range(0, chunk, HEAD_DIM):
            a = acc[:, h0:h0 + HEAD_DIM]
            col = c0 + h0
            if col < n_rot_cols:
                a = a * cos + pltpu.roll(a, HEAD_DIM // 2, 1) * sin
            elif scaled_cols[0] <= col < scaled_cols[1]:
                a = a * scale
            o_ref[:, col:col + HEAD_DIM] = a.astype(BF16)


def _inproj(x2d, g, w_all, layer, cos_t, sin_t, *, seq, n_rot_cols, scaled_cols,
            tm=512, chunk=512):
    t, d = x2d.shape
    n = w_all.shape[2]
    assert t % tm == 0 and seq % tm == 0 and n % chunk == 0 and chunk % HEAD_DIM == 0
    pos_blocks = seq // tm
    return pl.pallas_call(
        functools.partial(_inproj_kernel, n_rot_cols=n_rot_cols, scaled_cols=scaled_cols,
                          scale=HEAD_DIM ** -0.5 * LOG2_E, chunk=chunk),
        grid=(t // tm,),
        in_specs=[
            pl.BlockSpec((tm, d), lambda i: (i, 0)),
            pl.BlockSpec((1, d), lambda i: (0, 0)),
            pl.BlockSpec((None, d, n), lambda i: (layer, 0, 0), pipeline_mode=pl.Buffered(1)),
            pl.BlockSpec((tm, HEAD_DIM), lambda i: (i % pos_blocks, 0)),
            pl.BlockSpec((tm, HEAD_DIM), lambda i: (i % pos_blocks, 0)),
        ],
        out_specs=pl.BlockSpec((tm, n), lambda i: (i, 0)),
        out_shape=jax.ShapeDtypeStruct((t, n), BF16),
        scratch_shapes=[pltpu.VMEM((d, n), BF16)],
        compiler_params=pltpu.CompilerParams(
            dimension_semantics=("arbitrary",), vmem_limit_bytes=VMEM_LIMIT),
        name="inproj",
    )(x2d, g, w_all, cos_t, sin_t)


def _ret_kernel(q_ref, k_ref, v_ref, gate_ref, dmat_ref, qdec_ref, kdec_ref, sdec_ref, gn_ref,
                *rest, n_heads, n_trips, n_cast):
    hd = HEAD_DIM
    o_ref = rest[n_cast]
    for src_ref, dst_ref in zip(rest[:n_cast], rest[n_cast + 1:]):
        dst_ref[...] = src_ref[...].astype(BF16)

    def step(r, hh, state):
        c = slice(hh * hd, (hh + 1) * hd)
        q = q_ref[pl.ds(r, RET_BLOCK), c]
        k = k_ref[pl.ds(r, RET_BLOCK), c]
        v = v_ref[pl.ds(r, RET_BLOCK), c]
        s = lax.dot_general(q, k, _NT, preferred_element_type=F32) * dmat_ref[hh]
        intra = jnp.dot(s.astype(BF16), v, preferred_element_type=F32)
        qd = (q.astype(F32) * qdec_ref[hh]).astype(BF16)
        cross = jnp.dot(qd, state.astype(BF16), preferred_element_type=F32)
        kd = (k.astype(F32) * kdec_ref[hh]).astype(BF16)
        kv = lax.dot_general(kd, v, _TN, preferred_element_type=F32)
        o = intra + cross
        mu = jnp.mean(o, axis=-1, keepdims=True)
        oc = o - mu
        var = jnp.mean(oc * oc, axis=-1, keepdims=True)
        y = oc * lax.rsqrt(var + EPS) * gn_ref[hh]
        gate = gate_ref[pl.ds(r, RET_BLOCK), c].astype(F32)
        o_ref[pl.ds(r, RET_BLOCK), c] = (y * (gate * _sigmoid(gate))).astype(BF16)
        return state * sdec_ref[hh] + kv

    def body(n, states):
        states = list(states)
        for u in range(RET_UNROLL):
            r = pl.multiple_of((n * RET_UNROLL + u) * RET_BLOCK, RET_BLOCK)
            for hh in range(n_heads):
                states[hh] = step(r, hh, states[hh])
        return tuple(states)

    lax.fori_loop(0, n_trips, body,
                  tuple(jnp.zeros((hd, hd), F32) for _ in range(n_heads)))


def _retention(proj, dmat, qdec, kdec, sdec, gn, cast_weights, layer, *,
               n_heads, col_q, col_k, col_v, col_g):
    b, s, _ = proj.shape
    w = n_heads * HEAD_DIM
    assert s % (RET_BLOCK * RET_UNROLL) == 0
    assert col_q % w == 0 and col_k % w == 0 and col_v % w == 0 and col_g % w == 0
    assert all(a.shape[1] % (b * 16) == 0 for a in cast_weights)

    def col(c0):
        return pl.BlockSpec((None, s, w), lambda bi: (bi, 0, c0 // w))

    def table(a):
        return pl.BlockSpec(a.shape, lambda bi: (0, 0, 0))

    def slab_in(a):
        return pl.BlockSpec((None, a.shape[1] // b, a.shape[2]), lambda bi: (layer, bi, 0))

    def slab_out(a):
        return pl.BlockSpec((a.shape[1] // b, a.shape[2]), lambda bi: (bi, 0))

    outs = pl.pallas_call(
        functools.partial(_ret_kernel, n_heads=n_heads, n_trips=s // (RET_BLOCK * RET_UNROLL),
                          n_cast=len(cast_weights)),
        grid=(b,),
        in_specs=[col(col_q), col(col_k), col(col_v), col(col_g),
                  table(dmat), table(qdec), table(kdec), table(sdec), table(gn)]
                 + [slab_in(a) for a in cast_weights],
        out_specs=[pl.BlockSpec((None, s, w), lambda bi: (bi, 0, 0))]
                  + [slab_out(a) for a in cast_weights],
        out_shape=[jax.ShapeDtypeStruct((b, s, w), BF16)]
                  + [jax.ShapeDtypeStruct(a.shape[1:], BF16) for a in cast_weights],
        compiler_params=pltpu.CompilerParams(
            dimension_semantics=("arbitrary",), vmem_limit_bytes=VMEM_LIMIT),
        name="retention",
    )(proj, proj, proj, proj, dmat, qdec, kdec, sdec, gn, *cast_weights)
    return outs[0], outs[1:]


def _sb_kernel(q_ref, k_ref, v_ref, tri_ref, gn_ref, o_ref, acc_ref, carry_ref, *, n_heads):
    qi = pl.program_id(1)
    t = SB_TILE
    hd = HEAD_DIM
    row = lax.broadcasted_iota(jnp.int32, (t, t), 0)
    colm = lax.broadcasted_iota(jnp.int32, (t, t), 1)
    valid = colm < row

    def tile(j, masked):
        r = pl.multiple_of(j * t, t)
        heads = [slice(hh * hd, (hh + 1) * hd) for hh in range(n_heads)]
        keep, logb, rows, ws = {}, {}, {}, {}

        def scores(hh):
            c = heads[hh]
            z = lax.dot_general(q_ref[:, c], k_ref[pl.ds(r, t), c], _NT,
                                preferred_element_type=F32)
            sp = jnp.maximum(z, jnp.log2(1.0 + jnp.exp2(jnp.minimum(z, SB_EXP_CLAMP))))
            spm = jnp.where(valid, sp, 0.0) if masked else sp
            keep[hh] = spm.astype(BF16)
            logb[hh] = z - sp
            rows[hh] = jnp.sum(spm, axis=-1, keepdims=True)

        def weights(hh):
            after = jnp.dot(keep[hh], tri_ref[...], preferred_element_type=F32)
            w = jnp.exp2(logb[hh] - after)
            if masked:
                w = jnp.where(valid, w, 0.0)
            ws[hh] = w.astype(BF16)

        def accumulate(hh):
            c = heads[hh]
            pv = jnp.dot(ws[hh], v_ref[pl.ds(r, t), c], preferred_element_type=F32)
            if masked:
                acc_ref[:, c] = pv
                carry_ref[:, c] = jnp.broadcast_to(rows[hh], (t, hd))
            else:
                carry = carry_ref[:, c]
                acc_ref[:, c] += jnp.exp2(-carry) * pv
                carry_ref[:, c] = carry + rows[hh]

        for stage in (scores, weights, accumulate):
            for hh in range(n_heads):
                stage(hh)

    tile(qi, True)

    def more(state):
        it, least = state
        return (it < qi) & (least < SB_DEAD_LOG2)

    def body(state):
        it, _ = state
        tile(qi - 1 - it, False)
        return it + 1, jnp.min(carry_ref[...])

    lax.while_loop(more, body, (jnp.int32(0), jnp.min(carry_ref[...])))

    for hh in range(n_heads):
        c = slice(hh * hd, (hh + 1) * hd)
        acc = acc_ref[:, c]
        y = acc * lax.rsqrt(jnp.mean(acc * acc, axis=-1, keepdims=True) + EPS)
        o_ref[:, c] = (y * gn_ref[:, c]).astype(BF16)


def _stick_breaking(proj, tri, gn, *, n_heads, col_q, col_k, col_v):
    b, s, _ = proj.shape
    w = n_heads * HEAD_DIM
    t = SB_TILE
    assert s % t == 0 and col_q % w == 0 and col_k % w == 0 and col_v % w == 0
    return pl.pallas_call(
        functools.partial(_sb_kernel, n_heads=n_heads),
        grid=(b, s // t),
        in_specs=[
            pl.BlockSpec((None, t, w), lambda bi, qi: (bi, qi, col_q // w)),
            pl.BlockSpec((None, s, w), lambda bi, qi: (bi, 0, col_k // w)),
            pl.BlockSpec((None, s, w), lambda bi, qi: (bi, 0, col_v // w)),
            pl.BlockSpec(tri.shape, lambda bi, qi: (0, 0)),
            pl.BlockSpec((1, w), lambda bi, qi: (0, 0)),
        ],
        out_specs=pl.BlockSpec((None, t, w), lambda bi, qi: (bi, qi, 0)),
        out_shape=jax.ShapeDtypeStruct((b, s, w), BF16),
        scratch_shapes=[pltpu.VMEM((t, w), F32), pltpu.VMEM((t, w), F32)],
        compiler_params=pltpu.CompilerParams(
            dimension_semantics=("arbitrary", "arbitrary"), vmem_limit_bytes=VMEM_LIMIT),
        name="stick_breaking",
    )(proj, proj, proj, tri, gn)


def _mix_ffn_kernel(x_ref, ret_ref, sb_ref, wo_ref, g2_ref, wg_ref, wu_ref, wd_ref, fg_ref,
                    o_ref, acc_ref, h_ref, *, final_norm, tf):
    ka = ret_ref.shape[1]
    tm = x_ref.shape[0]
    ff = wd_ref.shape[0]
    for r0 in range(0, tm, MIX_ROWS):
        rows = slice(r0, r0 + MIX_ROWS)
        x1 = (x_ref[rows, :]
              + jnp.dot(ret_ref[rows, :], wo_ref[:ka, :], preferred_element_type=F32)
              + jnp.dot(sb_ref[rows, :], wo_ref[ka:, :], preferred_element_type=F32))
        acc_ref[rows, :] = x1
        y = x1 * lax.rsqrt(jnp.mean(x1 * x1, axis=-1, keepdims=True) + EPS)
        h_ref[rows, :] = (y * g2_ref[...]).astype(BF16)

    def ff_tile(f0, width):
        gt = jnp.dot(h_ref[...], wg_ref[:, pl.ds(f0, width)], preferred_element_type=F32)
        up = jnp.dot(h_ref[...], wu_ref[:, pl.ds(f0, width)], preferred_element_type=F32)
        a = (gt * _sigmoid(gt) * up).astype(BF16)
        acc_ref[...] += jnp.dot(a, wd_ref[pl.ds(f0, width), :], preferred_element_type=F32)

    def ff_body(f, _):
        ff_tile(pl.multiple_of(f * tf, tf), tf)
        return 0

    n_full = ff // tf
    lax.fori_loop(0, n_full, ff_body, 0)
    if ff % tf:
        ff_tile(n_full * tf, ff % tf)
    x2 = acc_ref[...]
    if final_norm:
        y = x2 * lax.rsqrt(jnp.mean(x2 * x2, axis=-1, keepdims=True) + EPS)
        o_ref[...] = y * fg_ref[...]
    else:
        o_ref[...] = x2


def _mix_ffn(x2d, ret2d, sb2d, wo, g2, wg, wu, wd, fg, *, final_norm, tm=1024, tf=FFN_TILE):
    t, d = x2d.shape
    ka, kb = ret2d.shape[1], sb2d.shape[1]
    ff = wg.shape[1]
    assert t % tm == 0 and tm % MIX_ROWS == 0 and wo.shape[0] == ka + kb
    assert tf % 256 == 0 and (ff % tf) % 256 == 0

    def resident(a):
        return pl.BlockSpec(a.shape, lambda i: (0, 0), pipeline_mode=pl.Buffered(1))

    return pl.pallas_call(
        functools.partial(_mix_ffn_kernel, final_norm=final_norm, tf=tf),
        grid=(t // tm,),
        in_specs=[
            pl.BlockSpec((tm, d), lambda i: (i, 0)),
            pl.BlockSpec((tm, ka), lambda i: (i, 0)),
            pl.BlockSpec((tm, kb), lambda i: (i, 0)),
            resident(wo),
            pl.BlockSpec((1, d), lambda i: (0, 0)),
            resident(wg), resident(wu), resident(wd),
            pl.BlockSpec((1, d), lambda i: (0, 0)),
        ],
        out_specs=pl.BlockSpec((tm, d), lambda i: (i, 0)),
        out_shape=jax.ShapeDtypeStruct((t, d), F32),
        scratch_shapes=[pltpu.VMEM((tm, d), F32), pltpu.VMEM((tm, d), BF16)],
        compiler_params=pltpu.CompilerParams(
            dimension_semantics=("arbitrary",), vmem_limit_bytes=VMEM_LIMIT),
        name="mix_ffn",
    )(x2d, ret2d, sb2d, wo, g2, wg, wu, wd, fg)


def _rotary_tables(seq):
    d = HEAD_DIM
    inv_freq = 1.0 / (ROPE_BASE ** (jnp.arange(0, d, 2, dtype=F32) / d))
    ang = jnp.arange(seq, dtype=F32)[:, None] * inv_freq[None, :]
    cos = jnp.cos(ang)
    sin = jnp.sin(ang)
    return jnp.concatenate([cos, cos], axis=-1), jnp.concatenate([-sin, sin], axis=-1)


def _retention_tables(n_heads):
    c = RET_BLOCK
    scale = HEAD_DIM ** -0.5
    log_g = jnp.log1p(-jnp.exp2(-5.0 - jnp.arange(n_heads, dtype=F32)))
    i = jnp.arange(c, dtype=F32)
    diff = i[:, None] - i[None, :]
    ci = jnp.arange(c) // RET_CHUNK
    same = ci[:, None] == ci[None, :]
    earlier = ci[None, :] < ci[:, None]
    dist = jnp.where(same, jnp.abs(diff), diff)
    dmat = jnp.where((same | earlier)[None], jnp.exp(log_g[:, None, None] * dist[None]), 0.0) * scale
    ones = jnp.ones((1, 1, HEAD_DIM), F32)
    qdec = jnp.exp(log_g[:, None] * (i + 1.0))[..., None] * ones
    kdec = jnp.exp(log_g[:, None] * (c - 1.0 - i))[..., None] * scale * ones
    sdec = jnp.exp(log_g * c)[:, None, None] * ones
    return dmat.astype(F32), qdec, kdec, sdec


def _after_matrix():
    j = jnp.arange(SB_TILE)
    return (j[:, None] > j[None, :]).astype(BF16)


def kernel(x, norm1_g, w_in, ret_norm_g, sb_norm_g, w_out, norm2_g, w_gate, w_up, w_down, final_g):
    b, s, d = x.shape
    depth = w_in.shape[0]
    ret_w = ret_norm_g.shape[1]
    sb_w = sb_norm_g.shape[1]
    n_ret, n_sb = ret_w // HEAD_DIM, sb_w // HEAD_DIM
    c_rq, c_rk, c_rv, c_rg = 0, ret_w, 2 * ret_w, 3 * ret_w
    c_sq, c_sk, c_sv = 4 * ret_w, 4 * ret_w + sb_w, 4 * ret_w + 2 * sb_w

    cos_t, sin_t = _rotary_tables(s)
    dmat, qdec, kdec, sdec = _retention_tables(n_ret)
    tri = _after_matrix()

    xc = x.reshape(b * s, d)
    for l in range(depth):
        proj = _inproj(xc, norm1_g[l][None], w_in, l, cos_t, sin_t,
                       seq=s, n_rot_cols=2 * ret_w, scaled_cols=(c_sq, c_sk))
        proj = proj.reshape(b, s, -1)
        ret, (wo_b, wg_b, wu_b, wd_b) = _retention(
            proj, dmat, qdec, kdec, sdec, ret_norm_g[l].reshape(n_ret, 1, HEAD_DIM),
            (w_out, w_gate, w_up, w_down), l,
            n_heads=n_ret, col_q=c_rq, col_k=c_rk, col_v=c_rv, col_g=c_rg)
        sb = _stick_breaking(proj, tri, sb_norm_g[l][None],
                             n_heads=n_sb, col_q=c_sq, col_k=c_sk, col_v=c_sv)
        xc = _mix_ffn(xc, ret.reshape(b * s, ret_w), sb.reshape(b * s, sb_w),
                      wo_b, norm2_g[l][None], wg_b, wu_b, wd_b, final_g[None],
                      final_norm=(l == depth - 1))
    return xc.reshape(b, s, d)
```

```python
import functools

import jax
import jax.numpy as jnp
from jax import lax
from jax.experimental import pallas as pl
from jax.experimental.pallas import tpu as pltpu

HEAD_DIM = 128
ROPE_BASE = 10000.0
EPS = 1e-6
LOG2_E = 1.4426950408889634
RET_BLOCK = 128
RET_CHUNK = 64
RET_UNROLL = 4
SB_TILE = 256
FFN_TILE = 512
MIX_ROWS = 256
SB_DEAD_LOG2 = 151.0
SB_EXP_CLAMP = 126.0
VMEM_LIMIT = 56 * 1024 * 1024

F32 = jnp.float32
BF16 = jnp.bfloat16
_NT = (((1,), (1,)), ((), ()))
_TN = (((0,), (0,)), ((), ()))


def _sigmoid(x):
    return 1.0 / (1.0 + jnp.exp(-x))


def _inproj_kernel(x_ref, g_ref, w_ref, cos_ref, sin_ref, o_ref, wb_ref, *,
                   n_rot_cols, scaled_cols, scale, chunk):
    @pl.when(pl.program_id(0) == 0)
    def _():
        for c0 in range(0, wb_ref.shape[1], chunk):
            wb_ref[:, c0:c0 + chunk] = w_ref[:, c0:c0 + chunk].astype(BF16)

    xf = x_ref[...]
    y = xf * lax.rsqrt(jnp.mean(xf * xf, axis=-1, keepdims=True) + EPS)
    h = (y * g_ref[...]).astype(BF16)
    cos = cos_ref[...]
    sin = sin_ref[...]
    for c0 in range(0, wb_ref.shape[1], chunk):
        acc = jnp.dot(h, wb_ref[:, c0:c0 + chunk], preferred_element_type=F32)
        for h0 in range(0, chunk, HEAD_DIM):
            a = acc[:, h0:h0 + HEAD_DIM]
            col = c0 + h0
            if col < n_rot_cols:
                a = a * cos + pltpu.roll(a, HEAD_DIM // 2, 1) * sin
            elif scaled_cols[0] <= col < scaled_cols[1]:
                a = a * scale
            o_ref[:, col:col + HEAD_DIM] = a.astype(BF16)


def _inproj(x2d, g, w_all, layer, cos_t, sin_t, *, seq, n_rot_cols, scaled_cols,
            tm=512, chunk=512):
    t, d = x2d.shape
    n = w_all.shape[2]
    assert t % tm == 0 and seq % tm == 0 and n % chunk == 0 and chunk % HEAD_DIM == 0
    pos_blocks = seq // tm
    return pl.pallas_call(
        functools.partial(_inproj_kernel, n_rot_cols=n_rot_cols, scaled_cols=scaled_cols,
                          scale=HEAD_DIM ** -0.5 * LOG2_E, chunk=chunk),
        grid=(t // tm,),
        in_specs=[
            pl.BlockSpec((tm, d), lambda i: (i, 0)),
            pl.BlockSpec((1, d), lambda i: (0, 0)),
            pl.BlockSpec((None, d, n), lambda i: (layer, 0, 0), pipeline_mode=pl.Buffered(1)),
            pl.BlockSpec((tm, HEAD_DIM), lambda i: (i % pos_blocks, 0)),
            pl.BlockSpec((tm, HEAD_DIM), lambda i: (i % pos_blocks, 0)),
        ],
        out_specs=pl.BlockSpec((tm, n), lambda i: (i, 0)),
        out_shape=jax.ShapeDtypeStruct((t, n), BF16),
        scratch_shapes=[pltpu.VMEM((d, n), BF16)],
        compiler_params=pltpu.CompilerParams(
            dimension_semantics=("arbitrary",), vmem_limit_bytes=VMEM_LIMIT),
        name="inproj",
    )(x2d, g, w_all, cos_t, sin_t)


def _ret_kernel(q_ref, k_ref, v_ref, gate_ref, dmat_ref, qdec_ref, kdec_ref, sdec_ref, gn_ref,
                *rest, n_heads, n_trips, n_cast):
    hd = HEAD_DIM
    o_ref = rest[n_cast]
    for src_ref, dst_ref in zip(rest[:n_cast], rest[n_cast + 1:]):
        dst_ref[...] = src_ref[...].astype(BF16)

    def step(r, hh, state):
        c = slice(hh * hd, (hh + 1) * hd)
        q = q_ref[pl.ds(r, RET_BLOCK), c]
        k = k_ref[pl.ds(r, RET_BLOCK), c]
        v = v_ref[pl.ds(r, RET_BLOCK), c]
        s = lax.dot_general(q, k, _NT, preferred_element_type=F32) * dmat_ref[hh]
        intra = jnp.dot(s.astype(BF16), v, preferred_element_type=F32)
        qd = (q.astype(F32) * qdec_ref[hh]).astype(BF16)
        cross = jnp.dot(qd, state.astype(BF16), preferred_element_type=F32)
        kd = (k.astype(F32) * kdec_ref[hh]).astype(BF16)
        kv = lax.dot_general(kd, v, _TN, preferred_element_type=F32)
        o = intra + cross
        mu = jnp.mean(o, axis=-1, keepdims=True)
        oc = o - mu
        var = jnp.mean(oc * oc, axis=-1, keepdims=True)
        y = oc * lax.rsqrt(var + EPS) * gn_ref[hh]
        gate = gate_ref[pl.ds(r, RET_BLOCK), c].astype(F32)
        o_ref[pl.ds(r, RET_BLOCK), c] = (y * (gate * _sigmoid(gate))).astype(BF16)
        return state * sdec_ref[hh] + kv

    def body(n, states):
        states = list(states)
        for u in range(RET_UNROLL):
            r = pl.multiple_of((n * RET_UNROLL + u) * RET_BLOCK, RET_BLOCK)
            for hh in range(n_heads):
                states[hh] = step(r, hh, states[hh])
        return tuple(states)

    lax.fori_loop(0, n_trips, body,
                  tuple(jnp.zeros((hd, hd), F32) for _ in range(n_heads)))


def _retention(proj, dmat, qdec, kdec, sdec, gn, cast_weights, layer, *,
               n_heads, col_q, col_k, col_v, col_g):
    b, s, _ = proj.shape
    w = n_heads * HEAD_DIM
    assert s % (RET_BLOCK * RET_UNROLL) == 0
    assert col_q % w == 0 and col_k % w == 0 and col_v % w == 0 and col_g % w == 0
    assert all(a.shape[1] % (b * 16) == 0 for a in cast_weights)

    def col(c0):
        return pl.BlockSpec((None, s, w), lambda bi: (bi, 0, c0 // w))

    def table(a):
        return pl.BlockSpec(a.shape, lambda bi: (0, 0, 0))

    def slab_in(a):
        return pl.BlockSpec((None, a.shape[1] // b, a.shape[2]), lambda bi: (layer, bi, 0))

    def slab_out(a):
        return pl.BlockSpec((a.shape[1] // b, a.shape[2]), lambda bi: (bi, 0))

    outs = pl.pallas_call(
        functools.partial(_ret_kernel, n_heads=n_heads, n_trips=s // (RET_BLOCK * RET_UNROLL),
                          n_cast=len(cast_weights)),
        grid=(b,),
        in_specs=[col(col_q), col(col_k), col(col_v), col(col_g),
                  table(dmat), table(qdec), table(kdec), table(sdec), table(gn)]
                 + [slab_in(a) for a in cast_weights],
        out_specs=[pl.BlockSpec((None, s, w), lambda bi: (bi, 0, 0))]
                  + [slab_out(a) for a in cast_weights],
        out_shape=[jax.ShapeDtypeStruct((b, s, w), BF16)]
                  + [jax.ShapeDtypeStruct(a.shape[1:], BF16) for a in cast_weights],
        compiler_params=pltpu.CompilerParams(
            dimension_semantics=("arbitrary",), vmem_limit_bytes=VMEM_LIMIT),
        name="retention",
    )(proj, proj, proj, proj, dmat, qdec, kdec, sdec, gn, *cast_weights)
    return outs[0], outs[1:]


def _sb_kernel(q_ref, k_ref, v_ref, tri_ref, gn_ref, o_ref, acc_ref, carry_ref, *, n_heads):
    qi = pl.program_id(1)
    t = SB_TILE
    hd = HEAD_DIM
    row = lax.broadcasted_iota(jnp.int32, (t, t), 0)
    colm = lax.broadcasted_iota(jnp.int32, (t, t), 1)
    valid = colm < row

    def sweep(tiles):
        heads = [slice(hh * hd, (hh + 1) * hd) for hh in range(n_heads)]
        units = [(pl.multiple_of(j * t, t), masked, hh)
                 for j, masked in tiles for hh in range(n_heads)]
        keep, logb, rows, ws = [], [], [], []

        for r, masked, hh in units:
            c = heads[hh]
            z = lax.dot_general(q_ref[:, c], k_ref[pl.ds(r, t), c], _NT,
                                preferred_element_type=F32)
            sp = jnp.maximum(z, jnp.log2(1.0 + jnp.exp2(jnp.minimum(z, SB_EXP_CLAMP))))
            spm = jnp.where(valid, sp, 0.0) if masked else sp
            keep.append(spm.astype(BF16))
            logb.append(z - sp)
            rows.append(jnp.sum(spm, axis=-1, keepdims=True))

        for u, (r, masked, hh) in enumerate(units):
            after = jnp.dot(keep[u], tri_ref[...], preferred_element_type=F32)
            w = jnp.exp2(logb[u] - after)
            if masked:
                w = jnp.where(valid, w, 0.0)
            ws.append(w.astype(BF16))

        for u, (r, masked, hh) in enumerate(units):
            c = heads[hh]
            pv = jnp.dot(ws[u], v_ref[pl.ds(r, t), c], preferred_element_type=F32)
            if masked:
                acc_ref[:, c] = pv
                carry_ref[:, c] = jnp.broadcast_to(rows[u], (t, hd))
            else:
                carry = carry_ref[:, c]
                acc_ref[:, c] += jnp.exp2(-carry) * pv
                carry_ref[:, c] = carry + rows[u]

    @pl.when(qi == 0)
    def _():
        sweep([(qi, True)])

    @pl.when(qi > 0)
    def _():
        sweep([(qi, True), (qi - 1, False)])

    def more(state):
        it, least = state
        return (it < qi) & (least < SB_DEAD_LOG2)

    def body(state):
        it, _ = state
        sweep([(qi - 1 - it, False)])
        return it + 1, jnp.min(carry_ref[...])

    lax.while_loop(more, body, (jnp.int32(1), jnp.min(carry_ref[...])))

    for hh in range(n_heads):
        c = slice(hh * hd, (hh + 1) * hd)
        acc = acc_ref[:, c]
        y = acc * lax.rsqrt(jnp.mean(acc * acc, axis=-1, keepdims=True) + EPS)
        o_ref[:, c] = (y * gn_ref[:, c]).astype(BF16)


def _stick_breaking(proj, tri, gn, *, n_heads, col_q, col_k, col_v):
    b, s, _ = proj.shape
    w = n_heads * HEAD_DIM
    t = SB_TILE
    assert s % t == 0 and col_q % w == 0 and col_k % w == 0 and col_v % w == 0
    return pl.pallas_call(
        functools.partial(_sb_kernel, n_heads=n_heads),
        grid=(b, s // t),
        in_specs=[
            pl.BlockSpec((None, t, w), lambda bi, qi: (bi, qi, col_q // w)),
            pl.BlockSpec((None, s, w), lambda bi, qi: (bi, 0, col_k // w)),
            pl.BlockSpec((None, s, w), lambda bi, qi: (bi, 0, col_v // w)),
            pl.BlockSpec(tri.shape, lambda bi, qi: (0, 0)),
            pl.BlockSpec((1, w), lambda bi, qi: (0, 0)),
        ],
        out_specs=pl.BlockSpec((None, t, w), lambda bi, qi: (bi, qi, 0)),
        out_shape=jax.ShapeDtypeStruct((b, s, w), BF16),
        scratch_shapes=[pltpu.VMEM((t, w), F32), pltpu.VMEM((t, w), F32)],
        compiler_params=pltpu.CompilerParams(
            dimension_semantics=("arbitrary", "arbitrary"), vmem_limit_bytes=VMEM_LIMIT),
        name="stick_breaking",
    )(proj, proj, proj, tri, gn)


def _mix_ffn_kernel(x_ref, ret_ref, sb_ref, wo_ref, g2_ref, wg_ref, wu_ref, wd_ref, fg_ref,
                    o_ref, acc_ref, h_ref, *, final_norm, tf):
    ka = ret_ref.shape[1]
    tm = x_ref.shape[0]
    ff = wd_ref.shape[0]
    for r0 in range(0, tm, MIX_ROWS):
        rows = slice(r0, r0 + MIX_ROWS)
        x1 = (x_ref[rows, :]
              + jnp.dot(ret_ref[rows, :], wo_ref[:ka, :], preferred_element_type=F32)
              + jnp.dot(sb_ref[rows, :], wo_ref[ka:, :], preferred_element_type=F32))
        acc_ref[rows, :] = x1
        y = x1 * lax.rsqrt(jnp.mean(x1 * x1, axis=-1, keepdims=True) + EPS)
        h_ref[rows, :] = (y * g2_ref[...]).astype(BF16)

    def ff_tile(f0, width):
        gt = jnp.dot(h_ref[...], wg_ref[:, pl.ds(f0, width)], preferred_element_type=F32)
        up = jnp.dot(h_ref[...], wu_ref[:, pl.ds(f0, width)], preferred_element_type=F32)
        a = (gt * _sigmoid(gt) * up).astype(BF16)
        acc_ref[...] += jnp.dot(a, wd_ref[pl.ds(f0, width), :], preferred_element_type=F32)

    def ff_body(f, _):
        ff_tile(pl.multiple_of(f * tf, tf), tf)
        return 0

    n_full = ff // tf
    lax.fori_loop(0, n_full, ff_body, 0)
    if ff % tf:
        ff_tile(n_full * tf, ff % tf)
    x2 = acc_ref[...]
    if final_norm:
        y = x2 * lax.rsqrt(jnp.mean(x2 * x2, axis=-1, keepdims=True) + EPS)
        o_ref[...] = y * fg_ref[...]
    else:
        o_ref[...] = x2


def _mix_ffn(x2d, ret2d, sb2d, wo, g2, wg, wu, wd, fg, *, final_norm, tm=1024, tf=FFN_TILE):
    t, d = x2d.shape
    ka, kb = ret2d.shape[1], sb2d.shape[1]
    ff = wg.shape[1]
    assert t % tm == 0 and tm % MIX_ROWS == 0 and wo.shape[0] == ka + kb
    assert tf % 256 == 0 and (ff % tf) % 256 == 0

    def resident(a):
        return pl.BlockSpec(a.shape, lambda i: (0, 0), pipeline_mode=pl.Buffered(1))

    return pl.pallas_call(
        functools.partial(_mix_ffn_kernel, final_norm=final_norm, tf=tf),
        grid=(t // tm,),
        in_specs=[
            pl.BlockSpec((tm, d), lambda i: (i, 0)),
            pl.BlockSpec((tm, ka), lambda i: (i, 0)),
            pl.BlockSpec((tm, kb), lambda i: (i, 0)),
            resident(wo),
            pl.BlockSpec((1, d), lambda i: (0, 0)),
            resident(wg), resident(wu), resident(wd),
            pl.BlockSpec((1, d), lambda i: (0, 0)),
        ],
        out_specs=pl.BlockSpec((tm, d), lambda i: (i, 0)),
        out_shape=jax.ShapeDtypeStruct((t, d), F32),
        scratch_shapes=[pltpu.VMEM((tm, d), F32), pltpu.VMEM((tm, d), BF16)],
        compiler_params=pltpu.CompilerParams(
            dimension_semantics=("arbitrary",), vmem_limit_bytes=VMEM_LIMIT),
        name="mix_ffn",
    )(x2d, ret2d, sb2d, wo, g2, wg, wu, wd, fg)


def _rotary_tables(seq):
    d = HEAD_DIM
    inv_freq = 1.0 / (ROPE_BASE ** (jnp.arange(0, d, 2, dtype=F32) / d))
    ang = jnp.arange(seq, dtype=F32)[:, None] * inv_freq[None, :]
    cos = jnp.cos(ang)
    sin = jnp.sin(ang)
    return jnp.concatenate([cos, cos], axis=-1), jnp.concatenate([-sin, sin], axis=-1)


def _retention_tables(n_heads):
    c = RET_BLOCK
    scale = HEAD_DIM ** -0.5
    log_g = jnp.log1p(-jnp.exp2(-5.0 - jnp.arange(n_heads, dtype=F32)))
    i = jnp.arange(c, dtype=F32)
    diff = i[:, None] - i[None, :]
    ci = jnp.arange(c) // RET_CHUNK
    same = ci[:, None] == ci[None, :]
    earlier = ci[None, :] < ci[:, None]
    dist = jnp.where(same, jnp.abs(diff), diff)
    dmat = jnp.where((same | earlier)[None], jnp.exp(log_g[:, None, None] * dist[None]), 0.0) * scale
    ones = jnp.ones((1, 1, HEAD_DIM), F32)
    qdec = jnp.exp(log_g[:, None] * (i + 1.0))[..., None] * ones
    kdec = jnp.exp(log_g[:, None] * (c - 1.0 - i))[..., None] * scale * ones
    sdec = jnp.exp(log_g * c)[:, None, None] * ones
    return dmat.astype(F32), qdec, kdec, sdec


def _after_matrix():
    j = jnp.arange(SB_TILE)
    return (j[:, None] > j[None, :]).astype(BF16)


def kernel(x, norm1_g, w_in, ret_norm_g, sb_norm_g, w_out, norm2_g, w_gate, w_up, w_down, final_g):
    b, s, d = x.shape
    depth = w_in.shape[0]
    ret_w = ret_norm_g.shape[1]
    sb_w = sb_norm_g.shape[1]
    n_ret, n_sb = ret_w // HEAD_DIM, sb_w // HEAD_DIM
    c_rq, c_rk, c_rv, c_rg = 0, ret_w, 2 * ret_w, 3 * ret_w
    c_sq, c_sk, c_sv = 4 * ret_w, 4 * ret_w + sb_w, 4 * ret_w + 2 * sb_w

    cos_t, sin_t = _rotary_tables(s)
    dmat, qdec, kdec, sdec = _retention_tables(n_ret)
    tri = _after_matrix()

    xc = x.reshape(b * s, d)
    for l in range(depth):
        proj = _inproj(xc, norm1_g[l][None], w_in, l, cos_t, sin_t,
                       seq=s, n_rot_cols=2 * ret_w, scaled_cols=(c_sq, c_sk))
        proj = proj.reshape(b, s, -1)
        ret, (wo_b, wg_b, wu_b, wd_b) = _retention(
            proj, dmat, qdec, kdec, sdec, ret_norm_g[l].reshape(n_ret, 1, HEAD_DIM),
            (w_out, w_gate, w_up, w_down), l,
            n_heads=n_ret, col_q=c_rq, col_k=c_rk, col_v=c_rv, col_g=c_rg)
        sb = _stick_breaking(proj, tri, sb_norm_g[l][None],
                             n_heads=n_sb, col_q=c_sq, col_k=c_sk, col_v=c_sv)
        xc = _mix_ffn(xc, ret.reshape(b * s, ret_w), sb.reshape(b * s, sb_w),
                      wo_b, norm2_g[l][None], wg_b, wu_b, wd_b, final_g[None],
                      final_norm=(l == depth - 1))
    return xc.reshape(b, s, d)
```
